```python
import math
import jax, jax.numpy as jnp
from jax import lax
import numpy as np

D_MODEL = 1024
BATCH = 4
SEQ = 8192
DEPTH = 1

N_HEADS = 8
HEAD_DIM = 64
ATTN_WIDTH = N_HEADS * HEAD_DIM
DILATIONS = ((128, 1), (512, 4), (2048, 16))
BLK = 128
ROPE_THETA = 10000.0
GMLP_GROUPS = 4
GMLP_WIDTH = D_MODEL // 2
GMLP_GROUP_DIM = GMLP_WIDTH // GMLP_GROUPS
CHUNK = 128
IN_COLS = 4 * ATTN_WIDTH + 3 * GMLP_WIDTH
MIX_WIDTH = ATTN_WIDTH + GMLP_WIDTH
PLE_DIM = 256
EPS = 1e-6
NEG = -1e30

kernel_name = "hybrid_dilated_attn_gmlp_parallel_heads"


def _rmsnorm(x, g):
    xf = x.astype(jnp.float32)
    y = xf * lax.rsqrt(jnp.mean(xf * xf, axis=-1, keepdims=True) + EPS)
    return (y * g.astype(jnp.float32)).astype(x.dtype)


def _layernorm(x, g, b):
    xf = x.astype(jnp.float32)
    mu = jnp.mean(xf, axis=-1, keepdims=True)
    xc = xf - mu
    var = jnp.mean(xc * xc, axis=-1, keepdims=True)
    y = xc * lax.rsqrt(var + EPS)
    return (y * g.astype(jnp.float32) + b.astype(jnp.float32)).astype(x.dtype)


def _rope(t, positions):
    half = HEAD_DIM // 2
    inv_freq = jnp.exp(-math.log(ROPE_THETA) * jnp.arange(half, dtype=jnp.float32) / half)
    ang = positions.astype(jnp.float32)[..., None] * inv_freq
    c = jnp.cos(ang)[:, :, None, :]
    s = jnp.sin(ang)[:, :, None, :]
    tf = t.astype(jnp.float32)
    t1, t2 = tf[..., :half], tf[..., half:]
    return jnp.concatenate([t1 * c - t2 * s, t2 * c + t1 * s], axis=-1).astype(t.dtype)


def _dilated_branch(q, k, v, window, dilation):
    B, S, H, D = q.shape
    r = dilation
    L = S // r
    n_back = window // r
    Lp = ((L + BLK - 1) // BLK) * BLK
    nb = Lp // BLK

    def split(t):
        t = t.reshape(B, L, r, H, D).transpose(0, 2, 1, 3, 4).reshape(B * r, L, H, D)
        return jnp.pad(t, ((0, 0), (0, Lp - L), (0, 0), (0, 0)))

    qs = split(q).reshape(B * r, nb, BLK, H, D)
    kb = split(k).reshape(B * r, nb, BLK, H, D)
    vb = split(v).reshape(B * r, nb, BLK, H, D)
    k_prev = jnp.pad(kb, ((0, 0), (1, 0), (0, 0), (0, 0), (0, 0)))[:, :-1]
    v_prev = jnp.pad(vb, ((0, 0), (1, 0), (0, 0), (0, 0), (0, 0)))[:, :-1]
    kk = jnp.concatenate([k_prev, kb], axis=2)
    vv = jnp.concatenate([v_prev, vb], axis=2)

    scores = jnp.einsum('nbqhd,nbkhd->nbhqk', qs, kk).astype(jnp.float32) * (HEAD_DIM ** -0.5)
    dist = (jnp.arange(BLK)[:, None] + BLK) - jnp.arange(2 * BLK)[None, :]
    band = (dist >= 0) & (dist <= n_back)
    key_idx = jnp.arange(nb)[:, None] * BLK - BLK + jnp.arange(2 * BLK)[None, :]
    valid = band[None, :, :] & (key_idx >= 0)[:, None, :]
    scores = jnp.where(valid[None, :, None], scores, NEG)
    m = jnp.max(scores, axis=-1, keepdims=True)
    e = jnp.exp(scores - m)
    den = jnp.sum(e, axis=-1, keepdims=True)
    out = jnp.einsum('nbhqk,nbkhd->nbqhd', (e / den).astype(v.dtype), vv)
    lse = (m + jnp.log(den))[..., 0]
    lse = lse.transpose(0, 1, 3, 2).reshape(B * r, Lp, H)[:, :L]
    out = out.reshape(B * r, Lp, H, D)[:, :L]
    out = out.reshape(B, r, L, H, D).transpose(0, 2, 1, 3, 4).reshape(B, S, H, D)
    lse = lse.reshape(B, r, L, H).transpose(0, 2, 1, 3).reshape(B, S, H)
    return out, lse


def _dilated_attention(q, k, v):
    outs, lses = [], []
    for window, dilation in DILATIONS:
        o, l = _dilated_branch(q, k, v, window, dilation)
        outs.append(o)
        lses.append(l)
    w = jax.nn.softmax(jnp.stack(lses, axis=0), axis=0)
    o = jnp.sum(w[..., None] * jnp.stack(outs, axis=0).astype(jnp.float32), axis=0)
    return o.astype(q.dtype)


def _gmlp_spatial(u, v, w_s, b_s, ln_g, ln_b):
    B, S, _ = u.shape
    u = jax.nn.gelu(u)
    v = _layernorm(jax.nn.gelu(v), ln_g, ln_b)
    vr = v.reshape(B, S // CHUNK, CHUNK, GMLP_GROUPS, GMLP_GROUP_DIM)
    ws = w_s * jnp.tril(jnp.ones((CHUNK, CHUNK), dtype=w_s.dtype))[None]
    sv = jnp.einsum('gts,bcsgd->bctgd', ws, vr) + b_s.T[None, None, :, :, None]
    return u * sv.reshape(B, S, GMLP_WIDTH)


def setup_inputs(seed: int = 0) -> dict:
    key = jax.random.key(seed)
    ks = jax.random.split(key, 16)
    f32 = jnp.float32
    nrm = lambda k, shape, scale: jax.random.normal(k, shape, f32) * scale
    x = jax.random.normal(ks[0], (BATCH, SEQ, D_MODEL), f32)
    p = jax.random.normal(ks[1], (DEPTH, BATCH, SEQ, PLE_DIM), f32)
    positions = jnp.broadcast_to(jnp.arange(SEQ, dtype=jnp.int32)[None, :], (BATCH, SEQ))
    return {
        "x": x,
        "p": p,
        "positions": positions,
        "g_in": 1.0 + nrm(ks[2], (DEPTH, D_MODEL), 0.01),
        "w_in": nrm(ks[3], (DEPTH, D_MODEL, IN_COLS), D_MODEL ** -0.5),
        "q_norm": 1.0 + nrm(ks[4], (DEPTH, HEAD_DIM), 0.01),
        "k_norm": 1.0 + nrm(ks[5], (DEPTH, HEAD_DIM), 0.01),
        "w_spatial": nrm(ks[6], (DEPTH, GMLP_GROUPS, CHUNK, CHUNK), 0.5 * CHUNK ** -0.5),
        "b_spatial": 1.0 + nrm(ks[7], (DEPTH, GMLP_GROUPS, CHUNK), 0.1),
        "ln_v_g": 1.0 + nrm(ks[8], (DEPTH, GMLP_WIDTH), 0.01),
        "ln_v_b": nrm(ks[9], (DEPTH, GMLP_WIDTH), 0.01),
        "w_out": nrm(ks[10], (DEPTH, MIX_WIDTH, D_MODEL), MIX_WIDTH ** -0.5),
        "g_ple": 1.0 + nrm(ks[11], (DEPTH, D_MODEL), 0.01),
        "w_ple_gate": nrm(ks[12], (DEPTH, D_MODEL, D_MODEL), D_MODEL ** -0.5),
        "b_ple_gate": nrm(ks[13], (DEPTH, D_MODEL), 0.01),
        "w_ple_proj": nrm(ks[14], (DEPTH, PLE_DIM, D_MODEL), PLE_DIM ** -0.5),
    }


def reference(x, p, positions, g_in, w_in, q_norm, k_norm, w_spatial, b_spatial,
              ln_v_g, ln_v_b, w_out, g_ple, w_ple_gate, b_ple_gate, w_ple_proj):
    B, S, _ = x.shape
    h = x
    for i in range(DEPTH):
        a = _rmsnorm(h, g_in[i])
        z = jnp.einsum('bsd,de->bse', a, w_in[i])
        q, k, v, gate_b, u_a, v_a, gate_a = jnp.split(
            z, np.cumsum([ATTN_WIDTH] * 4 + [GMLP_WIDTH] * 2).tolist(), axis=-1)
        q = _rope(_rmsnorm(q.reshape(B, S, N_HEADS, HEAD_DIM), q_norm[i]), positions)
        k = _rope(_rmsnorm(k.reshape(B, S, N_HEADS, HEAD_DIM), k_norm[i]), positions)
        v = v.reshape(B, S, N_HEADS, HEAD_DIM)
        o_b = _dilated_attention(q, k, v).reshape(B, S, ATTN_WIDTH) * jax.nn.silu(gate_b)
        o_a = _gmlp_spatial(u_a, v_a, w_spatial[i], b_spatial[i], ln_v_g[i], ln_v_b[i]) * jax.nn.silu(gate_a)
        mixed = jnp.concatenate([o_b, o_a], axis=-1)
        h = h + jnp.einsum('bse,ed->bsd', mixed, w_out[i])
        gate = jax.nn.sigmoid(jnp.einsum('bsd,de->bse', _rmsnorm(h, g_ple[i]), w_ple_gate[i]) + b_ple_gate[i])
        h = h + gate * jnp.einsum('bsp,pd->bsd', p[i], w_ple_proj[i])
    return h
```

```python
import functools
import math

import jax
import jax.numpy as jnp
import numpy as np
from jax import lax
from jax.experimental import pallas as pl
from jax.experimental.pallas import tpu as pltpu

D_MODEL = 1024
N_HEADS = 8
HEAD_DIM = 64
ATTN_WIDTH = N_HEADS * HEAD_DIM
DILATIONS = ((128, 1), (512, 4), (2048, 16))
BLK = 128
ROPE_THETA = 10000.0
GMLP_GROUPS = 4
GMLP_WIDTH = D_MODEL // 2
CHUNK = 128
PLE_DIM = 256
EPS = 1e-6
NEG = -1e30

LANES = 128
N_PAIRS = ATTN_WIDTH // LANES
STAT_REP = LANES // N_HEADS
VMEM_LIMIT = 56 * 1024 * 1024

PROJ_ROWS = 512
ATTN_ROWS = 512
OUT_ROWS = 512

BF16 = jnp.bfloat16
F32 = jnp.float32


def _dot(a, b):
    return jnp.dot(a, b, preferred_element_type=F32)


def _dot_nt(a, b):
    return lax.dot_general(a, b, (((1,), (1,)), ((), ())), preferred_element_type=F32)


def _silu(x):
    return x * jax.nn.sigmoid(x)


def _proj_kernel(x_ref, pos_ref, invf_ref, gin_ref, w_ref, avg_ref, qg_ref, kg_ref, ws_ref, bs_ref,
                 lng_ref, lnb_ref, q_ref, k_ref, v_ref, sg_ref, oa_ref):
    x = x_ref[0]
    a = (x * lax.rsqrt(jnp.mean(x * x, axis=-1, keepdims=True) + EPS) * gin_ref[...]).astype(BF16)

    def cols(j):
        return _dot(a, w_ref[:, j * ATTN_WIDTH:(j + 1) * ATTN_WIDTH])

    ang = pos_ref[0] * invf_ref[...]
    cos = jnp.cos(ang)
    sin = jnp.sin(ang)
    lane = lax.broadcasted_iota(jnp.int32, (1, LANES), 1)
    first_half = (lane % HEAD_DIM) < (HEAD_DIM // 2)
    sin_signed = jnp.where(first_half, -sin, sin)

    def norm_rope(z, g_ref, scale):
        ms = _dot((z * z).astype(BF16), avg_ref[...])
        y = z * lax.rsqrt(ms + EPS) * g_ref[...]
        outs = []
        for p in range(N_PAIRS):
            yp = y[:, p * LANES:(p + 1) * LANES]
            swapped = jnp.where(first_half, pltpu.roll(yp, LANES - HEAD_DIM // 2, 1),
                                pltpu.roll(yp, HEAD_DIM // 2, 1))
            outs.append(yp * cos + swapped * sin_signed)
        r = jnp.concatenate(outs, axis=1)
        if scale != 1.0:
            r = r * scale
        return r.astype(BF16)

    q_ref[0] = norm_rope(cols(0), qg_ref, HEAD_DIM ** -0.5)
    k_ref[0] = norm_rope(cols(1), kg_ref, 1.0)
    v_ref[0] = cols(2).astype(BF16)
    sg_ref[0] = _silu(cols(3)).astype(BF16)

    u = jax.nn.gelu(cols(4))
    gv = jax.nn.gelu(cols(5))
    mu = jnp.mean(gv, axis=-1, keepdims=True)
    gc = gv - mu
    var = jnp.mean(gc * gc, axis=-1, keepdims=True)
    vn = (gc * lax.rsqrt(var + EPS) * lng_ref[...] + lnb_ref[...]).astype(BF16)
    ga = _silu(cols(6))

    r_i = lax.broadcasted_iota(jnp.int32, (CHUNK, CHUNK), 0)
    c_i = lax.broadcasted_iota(jnp.int32, (CHUNK, CHUNK), 1)
    tril = c_i <= r_i
    ws = [jnp.where(tril, ws_ref[g], 0.0).astype(BF16) for g in range(GMLP_GROUPS)]
    bs = bs_ref[...]
    for c in range(x.shape[0] // CHUNK):
        rows = slice(c * CHUNK, (c + 1) * CHUNK)
        sv = jnp.concatenate(
            [_dot(ws[g], vn[rows, g * LANES:(g + 1) * LANES]) for g in range(GMLP_GROUPS)], axis=1) + bs
        oa_ref[0, rows, :] = (u[rows] * sv * ga[rows]).astype(BF16)


def _proj_call(x, pos_f, inv_freq, g_in, w_in, avg, qg, kg, w_s, b_tab, ln_g, ln_b):
    B, S, D = x.shape
    T = PROJ_ROWS
    n_cols = w_in.shape[1]
    tok = lambda width: pl.BlockSpec((1, T, width), lambda b, i: (b, i, 0))
    const = lambda shape: pl.BlockSpec(shape, lambda b, i: (0,) * len(shape))
    out_sds = jax.ShapeDtypeStruct((B, S, ATTN_WIDTH), BF16)
    return pl.pallas_call(
        _proj_kernel,
        grid=(B, S // T),
        in_specs=[tok(D), tok(1), const((1, LANES)), const((1, D)), const((D, n_cols)),
                  const((ATTN_WIDTH, ATTN_WIDTH)), const((1, ATTN_WIDTH)), const((1, ATTN_WIDTH)),
                  const((GMLP_GROUPS, CHUNK, CHUNK)), const((CHUNK, GMLP_WIDTH)),
                  const((1, GMLP_WIDTH)), const((1, GMLP_WIDTH))],
        out_specs=[tok(ATTN_WIDTH)] * 5,
        out_shape=[out_sds] * 5,
        compiler_params=pltpu.CompilerParams(
            dimension_semantics=("arbitrary", "arbitrary"), vmem_limit_bytes=VMEM_LIMIT),
        name="proj",
    )(x, pos_f, inv_freq, g_in, w_in, avg, qg, kg, w_s, b_tab, ln_g, ln_b)


def _attn_kernel(bias_ref, q_ref, k_ref, v_ref, kp_ref, vp_ref, o_ref, l_ref):
    first_tile = (pl.program_id(1) == 0).astype(jnp.int32)
    lane = lax.broadcasted_iota(jnp.int32, (1, LANES), 1)
    in_a = lane < HEAD_DIM
    mask_a = in_a.astype(BF16)
    mask_b = 1.0 - mask_a
    group = lane // STAT_REP % N_PAIRS

    for s in range(q_ref.shape[1] // BLK):
        rows = slice(s * BLK, (s + 1) * BLK)
        bias = bias_ref[first_tile] if s == 0 else bias_ref[0]
        packed = None
        for p in range(N_PAIRS):
            lanes = slice(p * LANES, (p + 1) * LANES)
            qp = q_ref[0, rows, lanes]
            if s == 0:
                kk = jnp.concatenate([kp_ref[0, :, lanes], k_ref[0, rows, lanes]], axis=0)
                vv = jnp.concatenate([vp_ref[0, :, lanes], v_ref[0, rows, lanes]], axis=0)
            else:
                kk = k_ref[0, (s - 1) * BLK:(s + 1) * BLK, lanes]
                vv = v_ref[0, (s - 1) * BLK:(s + 1) * BLK, lanes]
            q2 = jnp.concatenate([qp * mask_a, qp * mask_b], axis=0)
            sc = _dot_nt(q2, kk) + bias
            m = jnp.max(sc, axis=-1, keepdims=True)
            e = jnp.exp(sc - m).astype(BF16)
            e2 = jnp.concatenate([e[:BLK], e[BLK:]], axis=1)
            ones_a = jnp.broadcast_to(mask_a, vv.shape)
            ones_b = jnp.broadcast_to(mask_b, vv.shape)
            rhs = jnp.concatenate([jnp.concatenate([vv * mask_a, ones_a], axis=1),
                                   jnp.concatenate([vv * mask_b, ones_b], axis=1)], axis=0)
            res = _dot(e2, rhs)
            num = res[:, :LANES]
            den = res[:, LANES:]
            o_ref[0, rows, lanes] = (num / den).astype(o_ref.dtype)
            lse = jnp.where(in_a, m[:BLK], m[BLK:]) + jnp.log(den)
            packed = lse if packed is None else jnp.where(group == p, lse, packed)
        l_ref[0, rows, :] = packed


def _attn_call(bias, q, k, v, dilation):
    B, S, W = q.shape
    r = dilation
    L = S // r
    R = min(ATTN_ROWS, L)
    view = lambda t: t.reshape(B, L, r * W)
    cur = pl.BlockSpec((1, R, W), lambda b, i, c: (b, i, c))
    prev = pl.BlockSpec((1, BLK, W), lambda b, i, c: (b, jnp.maximum(i * (R // BLK) - 1, 0), c))
    o, l = pl.pallas_call(
        _attn_kernel,
        grid=(B, L // R, r),
        in_specs=[pl.BlockSpec(bias.shape, lambda b, i, c: (0, 0, 0)), cur, cur, cur, prev, prev],
        out_specs=[cur, pl.BlockSpec((1, R, LANES), lambda b, i, c: (b, i, c))],
        out_shape=[jax.ShapeDtypeStruct((B, L, r * W), BF16),
                   jax.ShapeDtypeStruct((B, L, r * LANES), F32)],
        compiler_params=pltpu.CompilerParams(
            dimension_semantics=("arbitrary", "arbitrary", "arbitrary"), vmem_limit_bytes=VMEM_LIMIT),
        name=f"attn_d{r}",
    )(bias, view(q), view(k), view(v), view(k), view(v))
    return o.reshape(B, S, W), l.reshape(B, S, LANES)


def _out_kernel(x_ref, p_ref, o1_ref, o2_ref, o3_ref, l1_ref, l2_ref, l3_ref, sg_ref, oa_ref,
                exp_ref, wo_ref, gple_ref, wg_ref, bg_ref, wp_ref, out_ref):
    lses = [l1_ref[0], l2_ref[0], l3_ref[0]]
    m = jnp.maximum(jnp.maximum(lses[0], lses[1]), lses[2])
    es = [jnp.exp(l - m) for l in lses]
    inv = 1.0 / (es[0] + es[1] + es[2])
    ob = None
    for e, o_ref in zip(es, (o1_ref, o2_ref, o3_ref)):
        w = e * inv
        w_hi = w.astype(BF16)
        w_lo = (w - w_hi.astype(F32)).astype(BF16)
        w_full = _dot(jnp.concatenate([w_hi, w_lo], axis=1), exp_ref[...])
        t = w_full * o_ref[0].astype(F32)
        ob = t if ob is None else ob + t
    mb = (ob * sg_ref[0].astype(F32)).astype(BF16)

    h = x_ref[0] + _dot(mb, wo_ref[:ATTN_WIDTH, :]) + _dot(oa_ref[0], wo_ref[ATTN_WIDTH:, :])
    n = (h * lax.rsqrt(jnp.mean(h * h, axis=-1, keepdims=True) + EPS) * gple_ref[...]).astype(BF16)
    gate = jax.nn.sigmoid(_dot(n, wg_ref[...]) + bg_ref[...])
    out_ref[0] = h + gate * _dot(p_ref[0].astype(BF16), wp_ref[...])


def _out_call(x, p, os_, ls_, sg, oa, expand, w_out, g_ple, w_gate, b_gate, w_proj):
    B, S, D = x.shape
    T = OUT_ROWS
    tok = lambda width: pl.BlockSpec((1, T, width), lambda b, i: (b, i, 0))
    const = lambda shape: pl.BlockSpec(shape, lambda b, i: (0,) * len(shape))
    return pl.pallas_call(
        _out_kernel,
        grid=(B, S // T),
        in_specs=[tok(D), tok(PLE_DIM)] + [tok(ATTN_WIDTH)] * 3 + [tok(LANES)] * 3
        + [tok(ATTN_WIDTH), tok(GMLP_WIDTH), const(expand.shape), const(w_out.shape), const((1, D)),
           const(w_gate.shape), const((1, D)), const(w_proj.shape)],
        out_specs=tok(D),
        out_shape=jax.ShapeDtypeStruct((B, S, D), x.dtype),
        compiler_params=pltpu.CompilerParams(
            dimension_semantics=("arbitrary", "arbitrary"), vmem_limit_bytes=VMEM_LIMIT),
        name="out",
    )(x, p, *os_, *ls_, sg, oa, expand, w_out, g_ple, w_gate, b_gate, w_proj)


def _band_bias():
    qi = np.arange(BLK)[:, None]
    col = np.arange(2 * BLK)[None, :]
    dist = qi + BLK - col
    band = (dist >= 0) & (dist <= BLK)
    both = np.stack([band, band & (col >= BLK)])
    bias = np.where(both, 0.0, NEG).astype(np.float32)
    return jnp.asarray(np.concatenate([bias, bias], axis=1))


def _head_average():
    h = np.arange(ATTN_WIDTH) // HEAD_DIM
    return jnp.asarray((h[:, None] == h[None, :]).astype(np.float32) / HEAD_DIM, dtype=BF16)


def _stat_expand():
    lane = np.arange(LANES)
    g = lane // STAT_REP
    head = np.where(g < N_PAIRS, 2 * g, 2 * (g - N_PAIRS) + 1)
    pick = (lane % STAT_REP) == 0
    col_head = np.arange(ATTN_WIDTH) // HEAD_DIM
    e = ((head[:, None] == col_head[None, :]) & pick[:, None]).astype(np.float32)
    return jnp.asarray(np.concatenate([e, e], axis=0), dtype=BF16)


def _inv_freq_row():
    half = HEAD_DIM // 2
    inv = jnp.exp(-math.log(ROPE_THETA) * jnp.arange(half, dtype=F32) / half)
    return jnp.tile(inv, LANES // half)[None, :]


def kernel(x, p, positions, g_in, w_in, q_norm, k_norm, w_spatial, b_spatial, ln_v_g, ln_v_b, w_out,
           g_ple, w_ple_gate, b_ple_gate, w_ple_proj):
    B, S, _ = x.shape
    depth = w_in.shape[0]
    bias = _band_bias()
    avg = _head_average()
    expand = _stat_expand()
    inv_freq = _inv_freq_row()
    pos_f = positions.astype(F32)[..., None]
    h = x
    for i in range(depth):
        q, k, v, sg, oa = _proj_call(
            h, pos_f, inv_freq, g_in[i][None, :], w_in[i].astype(BF16), avg,
            jnp.tile(q_norm[i], N_HEADS)[None, :], jnp.tile(k_norm[i], N_HEADS)[None, :],
            w_spatial[i], jnp.repeat(b_spatial[i].T, LANES, axis=1),
            ln_v_g[i][None, :], ln_v_b[i][None, :])
        os_, ls_ = [], []
        for _, dilation in DILATIONS:
            o, l = _attn_call(bias, q, k, v, dilation)
            os_.append(o)
            ls_.append(l)
        h = _out_call(h, p[i], os_, ls_, sg, oa, expand, w_out[i].astype(BF16), g_ple[i][None, :],
                      w_ple_gate[i].astype(BF16), b_ple_gate[i][None, :], w_ple_proj[i].astype(BF16))
    return h
```

```python
import math

import jax
import jax.numpy as jnp
import numpy as np
from jax import lax
from jax.experimental import pallas as pl
from jax.experimental.pallas import tpu as pltpu

D_MODEL = 1024
N_HEADS = 8
HEAD_DIM = 64
ATTN_WIDTH = N_HEADS * HEAD_DIM
DILATIONS = ((128, 1), (512, 4), (2048, 16))
BLK = 128
ROPE_THETA = 10000.0
GMLP_GROUPS = 4
GMLP_WIDTH = D_MODEL // 2
CHUNK = 128
PLE_DIM = 256
EPS = 1e-6
NEG = -1e30

LANES = 128
MXU_DEPTH = 256
N_PAIRS = ATTN_WIDTH // LANES
STAT_REP = LANES // N_HEADS
VMEM_LIMIT = 56 * 1024 * 1024

PROJ_ROWS = 512
ATTN_ROWS = 512
OUT_ROWS = 512
PERM_ROWS = MXU_DEPTH
STRIDES = tuple(d for _, d in DILATIONS if d > 1)

BF16 = jnp.bfloat16
F32 = jnp.float32


def _dot(a, b):
    return jnp.dot(a, b, preferred_element_type=F32)


def _dot_nt(a, b):
    return lax.dot_general(a, b, (((1,), (1,)), ((), ())), preferred_element_type=F32)


def _silu(x):
    return x * jax.nn.sigmoid(x)


def _proj_kernel(x_ref, pos_ref, invf_ref, gin_ref, w_ref, avg_ref, qg_ref, kg_ref, ws_ref, bs_ref,
                 lng_ref, lnb_ref, d4_ref, d16_ref,
                 q_ref, k_ref, v_ref, q4_ref, k4_ref, v4_ref, q16_ref, k16_ref, v16_ref, sg_ref, oa_ref):
    x = x_ref[0]
    n_rows = x.shape[0]
    a = (x * lax.rsqrt(jnp.mean(x * x, axis=-1, keepdims=True) + EPS) * gin_ref[...]).astype(BF16)

    def cols(j):
        return _dot(a, w_ref[:, j * ATTN_WIDTH:(j + 1) * ATTN_WIDTH])

    ang = pos_ref[0] * invf_ref[...]
    cos = jnp.cos(ang)
    sin = jnp.sin(ang)
    lane = lax.broadcasted_iota(jnp.int32, (1, LANES), 1)
    first_half = (lane % HEAD_DIM) < (HEAD_DIM // 2)
    sin_signed = jnp.where(first_half, -sin, sin)

    def norm_rope(z, g_ref, scale):
        ms = _dot((z * z).astype(BF16), avg_ref[...])
        y = z * lax.rsqrt(ms + EPS) * g_ref[...]
        outs = []
        for p in range(N_PAIRS):
            yp = y[:, p * LANES:(p + 1) * LANES]
            swapped = jnp.where(first_half, pltpu.roll(yp, LANES - HEAD_DIM // 2, 1),
                                pltpu.roll(yp, HEAD_DIM // 2, 1))
            outs.append(yp * cos + swapped * sin_signed)
        r = jnp.concatenate(outs, axis=1)
        if scale != 1.0:
            r = r * scale
        return r.astype(BF16)

    def emit(z, nat_ref, regrouped):
        nat_ref[0] = z
        for j in range(n_rows // PERM_ROWS):
            piece = z[j * PERM_ROWS:(j + 1) * PERM_ROWS]
            for (d_ref, out_ref) in regrouped:
                r = out_ref.shape[1]
                n = PERM_ROWS // r
                cm = _dot(d_ref[...], piece).astype(BF16)
                for c in range(r):
                    out_ref[0, c, j * n:(j + 1) * n, :] = cm[c * n:(c + 1) * n]

    emit(norm_rope(cols(0), qg_ref, HEAD_DIM ** -0.5), q_ref, ((d4_ref, q4_ref), (d16_ref, q16_ref)))
    emit(norm_rope(cols(1), kg_ref, 1.0), k_ref, ((d4_ref, k4_ref), (d16_ref, k16_ref)))
    emit(cols(2).astype(BF16), v_ref, ((d4_ref, v4_ref), (d16_ref, v16_ref)))
    sg_ref[0] = _silu(cols(3)).astype(BF16)

    u = jax.nn.gelu(cols(4))
    gv = jax.nn.gelu(cols(5))
    mu = jnp.mean(gv, axis=-1, keepdims=True)
    gc = gv - mu
    var = jnp.mean(gc * gc, axis=-1, keepdims=True)
    vn = (gc * lax.rsqrt(var + EPS) * lng_ref[...] + lnb_ref[...]).astype(BF16)
    ga = _silu(cols(6))

    r_i = lax.broadcasted_iota(jnp.int32, (CHUNK, CHUNK), 0)
    c_i = lax.broadcasted_iota(jnp.int32, (CHUNK, CHUNK), 1)
    tril = c_i <= r_i
    ws = [jnp.where(tril, ws_ref[g], 0.0).astype(BF16) for g in range(GMLP_GROUPS)]
    bs = bs_ref[...]
    for c in range(n_rows // CHUNK):
        rows = slice(c * CHUNK, (c + 1) * CHUNK)
        sv = jnp.concatenate(
            [_dot(ws[g], vn[rows, g * LANES:(g + 1) * LANES]) for g in range(GMLP_GROUPS)], axis=1) + bs
        oa_ref[0, rows, :] = (u[rows] * sv * ga[rows]).astype(BF16)


def _proj_call(x, pos_f, inv_freq, g_in, w_in, avg, qg, kg, w_s, b_tab, ln_g, ln_b, perms):
    B, S, D = x.shape
    T = PROJ_ROWS
    n_cols = w_in.shape[1]
    W = ATTN_WIDTH
    tok = lambda width: pl.BlockSpec((1, T, width), lambda b, i: (b, i, 0))
    const = lambda shape: pl.BlockSpec(shape, lambda b, i: (0,) * len(shape))
    cls = lambda r: pl.BlockSpec((1, r, T // r, W), lambda b, i: (b, 0, i, 0))
    nat_sds = jax.ShapeDtypeStruct((B, S, W), BF16)
    cls_sds = lambda r: jax.ShapeDtypeStruct((B, r, S // r, W), BF16)
    r4, r16 = STRIDES
    return pl.pallas_call(
        _proj_kernel,
        grid=(B, S // T),
        in_specs=[tok(D), tok(1), const((1, LANES)), const((1, D)), const((D, n_cols)),
                  const((W, W)), const((1, W)), const((1, W)),
                  const((GMLP_GROUPS, CHUNK, CHUNK)), const((CHUNK, GMLP_WIDTH)),
                  const((1, GMLP_WIDTH)), const((1, GMLP_WIDTH)),
                  const((PERM_ROWS, PERM_ROWS)), const((PERM_ROWS, PERM_ROWS))],
        out_specs=[tok(W)] * 3 + [cls(r4)] * 3 + [cls(r16)] * 3 + [tok(W)] * 2,
        out_shape=[nat_sds] * 3 + [cls_sds(r4)] * 3 + [cls_sds(r16)] * 3 + [nat_sds] * 2,
        compiler_params=pltpu.CompilerParams(
            dimension_semantics=("arbitrary", "arbitrary"), vmem_limit_bytes=VMEM_LIMIT),
        name="proj",
    )(x, pos_f, inv_freq, g_in, w_in, avg, qg, kg, w_s, b_tab, ln_g, ln_b, perms[r4][0], perms[r16][0])


def _attn_kernel(bias_ref, q_ref, k_ref, v_ref, kp_ref, vp_ref, o_ref, l_ref):
    first_tile = (pl.program_id(1) == 0).astype(jnp.int32)
    lane = lax.broadcasted_iota(jnp.int32, (1, LANES), 1)
    in_a = lane < HEAD_DIM
    mask_a = in_a.astype(BF16)
    mask_b = 1.0 - mask_a
    group = lane // STAT_REP % N_PAIRS

    for s in range(q_ref.shape[0] // BLK):
        rows = slice(s * BLK, (s + 1) * BLK)
        bias = bias_ref[first_tile] if s == 0 else bias_ref[0]
        packed = None
        for p in range(N_PAIRS):
            lanes = slice(p * LANES, (p + 1) * LANES)
            qp = q_ref[rows, lanes]
            if s == 0:
                kk = jnp.concatenate([kp_ref[:, lanes], k_ref[rows, lanes]], axis=0)
                vv = jnp.concatenate([vp_ref[:, lanes], v_ref[rows, lanes]], axis=0)
            else:
                kk = k_ref[(s - 1) * BLK:(s + 1) * BLK, lanes]
                vv = v_ref[(s - 1) * BLK:(s + 1) * BLK, lanes]
            q2 = jnp.concatenate([qp * mask_a, qp * mask_b], axis=0)
            sc = _dot_nt(q2, kk) + bias
            m = jnp.max(sc, axis=-1, keepdims=True)
            e = jnp.exp(sc - m).astype(BF16)
            e2 = jnp.concatenate([e[:BLK], e[BLK:]], axis=1)
            ones_a = jnp.broadcast_to(mask_a, vv.shape)
            ones_b = jnp.broadcast_to(mask_b, vv.shape)
            rhs = jnp.concatenate([jnp.concatenate([vv * mask_a, ones_a], axis=1),
                                   jnp.concatenate([vv * mask_b, ones_b], axis=1)], axis=0)
            res = _dot(e2, rhs)
            num = res[:, :LANES]
            den = res[:, LANES:]
            o_ref[rows, lanes] = (num / den).astype(o_ref.dtype)
            lse = jnp.where(in_a, m[:BLK], m[BLK:]) + jnp.log(den)
            packed = lse if packed is None else jnp.where(group == p, lse, packed)
        l_ref[rows, :] = packed


def _attn_call(bias, q, k, v):
    B, r, L, W = q.shape
    R = min(ATTN_ROWS, L)
    cur = lambda width: pl.BlockSpec((None, None, R, width), lambda b, i, c: (b, c, i, 0))
    prev = pl.BlockSpec((None, None, BLK, W),
                        lambda b, i, c: (b, c, jnp.maximum(i * (R // BLK) - 1, 0), 0))
    return pl.pallas_call(
        _attn_kernel,
        grid=(B, L // R, r),
        in_specs=[pl.BlockSpec(bias.shape, lambda b, i, c: (0, 0, 0)), cur(W), cur(W), cur(W), prev, prev],
        out_specs=[cur(W), cur(LANES)],
        out_shape=[jax.ShapeDtypeStruct((B, r, L, W), BF16), jax.ShapeDtypeStruct((B, r, L, LANES), F32)],
        compiler_params=pltpu.CompilerParams(
            dimension_semantics=("arbitrary", "arbitrary", "arbitrary"), vmem_limit_bytes=VMEM_LIMIT),
        name=f"attn_d{r}",
    )(bias, q, k, v, k, v)


def _out_kernel(x_ref, p_ref, o1_ref, o4_ref, o16_ref, l1_ref, l4_ref, l16_ref, sg_ref, oa_ref,
                u4_ref, u16_ref, exp_ref, wo_ref, gple_ref, wg_ref, bg_ref, wp_ref, out_ref):
    n_rows = x_ref.shape[1]

    def to_token_order(ref, u_ref, split):
        r = ref.shape[0]
        n = PERM_ROWS // r
        pieces = []
        for j in range(n_rows // PERM_ROWS):
            cm = jnp.concatenate([ref[c, j * n:(j + 1) * n, :] for c in range(r)], axis=0)
            if split:
                hi = cm.astype(BF16)
                lo = (cm - hi.astype(F32)).astype(BF16)
                t = _dot(u_ref[...], jnp.concatenate([hi, lo], axis=1))
                pieces.append(t[:, :cm.shape[1]] + t[:, cm.shape[1]:])
            else:
                pieces.append(_dot(u_ref[...], cm))
        return jnp.concatenate(pieces, axis=0)

    lses = [l1_ref[0], to_token_order(l4_ref, u4_ref, True), to_token_order(l16_ref, u16_ref, True)]
    outs = [o1_ref[0].astype(F32), to_token_order(o4_ref, u4_ref, False),
            to_token_order(o16_ref, u16_ref, False)]
    m = jnp.maximum(jnp.maximum(lses[0], lses[1]), lses[2])
    es = [jnp.exp(l - m) for l in lses]
    inv = 1.0 / (es[0] + es[1] + es[2])
    ob = None
    for e, o in zip(es, outs):
        w = e * inv
        w_hi = w.astype(BF16)
        w_lo = (w - w_hi.astype(F32)).astype(BF16)
        w_full = _dot(jnp.concatenate([w_hi, w_lo], axis=1), exp_ref[...])
        ob = w_full * o if ob is None else ob + w_full * o
    mb = (ob * sg_ref[0].astype(F32)).astype(BF16)

    h = x_ref[0] + _dot(mb, wo_ref[:ATTN_WIDTH, :]) + _dot(oa_ref[0], wo_ref[ATTN_WIDTH:, :])
    n = (h * lax.rsqrt(jnp.mean(h * h, axis=-1, keepdims=True) + EPS) * gple_ref[...]).astype(BF16)
    gate = jax.nn.sigmoid(_dot(n, wg_ref[...]) + bg_ref[...])
    out_ref[0] = h + gate * _dot(p_ref[0].astype(BF16), wp_ref[...])


def _out_call(x, p, os_, ls_, sg, oa, perms, expand, w_out, g_ple, w_gate, b_gate, w_proj):
    B, S, D = x.shape
    T = OUT_ROWS
    tok = lambda width: pl.BlockSpec((1, T, width), lambda b, i: (b, i, 0))
    cls = lambda r, width: pl.BlockSpec((None, r, T // r, width), lambda b, i: (b, 0, i, 0))
    const = lambda shape: pl.BlockSpec(shape, lambda b, i: (0,) * len(shape))
    r4, r16 = STRIDES
    W = ATTN_WIDTH
    return pl.pallas_call(
        _out_kernel,
        grid=(B, S // T),
        in_specs=[tok(D), tok(PLE_DIM), tok(W), cls(r4, W), cls(r16, W), tok(LANES), cls(r4, LANES),
                  cls(r16, LANES), tok(W), tok(GMLP_WIDTH),
                  const((PERM_ROWS, PERM_ROWS)), const((PERM_ROWS, PERM_ROWS)), const(expand.shape),
                  const(w_out.shape), const((1, D)), const(w_gate.shape), const((1, D)), const(w_proj.shape)],
        out_specs=tok(D),
        out_shape=jax.ShapeDtypeStruct((B, S, D), x.dtype),
        compiler_params=pltpu.CompilerParams(
            dimension_semantics=("arbitrary", "arbitrary"), vmem_limit_bytes=VMEM_LIMIT),
        name="out",
    )(x, p, *os_, *ls_, sg, oa, perms[r4][1], perms[r16][1], expand, w_out, g_ple, w_gate, b_gate, w_proj)


def _band_bias():
    qi = np.arange(BLK)[:, None]
    col = np.arange(2 * BLK)[None, :]
    dist = qi + BLK - col
    band = (dist >= 0) & (dist <= BLK)
    both = np.stack([band, band & (col >= BLK)])
    bias = np.where(both, 0.0, NEG).astype(np.float32)
    return jnp.asarray(np.concatenate([bias, bias], axis=1))


def _head_average():
    h = np.arange(ATTN_WIDTH) // HEAD_DIM
    return jnp.asarray((h[:, None] == h[None, :]).astype(np.float32) / HEAD_DIM, dtype=BF16)


def _stat_expand():
    lane = np.arange(LANES)
    g = lane // STAT_REP
    head = np.where(g < N_PAIRS, 2 * g, 2 * (g - N_PAIRS) + 1)
    pick = (lane % STAT_REP) == 0
    col_head = np.arange(ATTN_WIDTH) // HEAD_DIM
    e = ((head[:, None] == col_head[None, :]) & pick[:, None]).astype(np.float32)
    return jnp.asarray(np.concatenate([e, e], axis=0), dtype=BF16)


def _class_perms(r):
    t = np.arange(PERM_ROWS)
    grouped_pos = (t % r) * (PERM_ROWS // r) + t // r
    d = np.zeros((PERM_ROWS, PERM_ROWS), np.float32)
    d[grouped_pos, t] = 1.0
    return jnp.asarray(d, dtype=BF16), jnp.asarray(d.T, dtype=BF16)


def _inv_freq_row():
    half = HEAD_DIM // 2
    inv = jnp.exp(-math.log(ROPE_THETA) * jnp.arange(half, dtype=F32) / half)
    return jnp.tile(inv, LANES // half)[None, :]


def kernel(x, p, positions, g_in, w_in, q_norm, k_norm, w_spatial, b_spatial, ln_v_g, ln_v_b, w_out,
           g_ple, w_ple_gate, b_ple_gate, w_ple_proj):
    B, S, _ = x.shape
    depth = w_in.shape[0]
    bias = _band_bias()
    avg = _head_average()
    expand = _stat_expand()
    inv_freq = _inv_freq_row()
    perms = {r: _class_perms(r) for r in STRIDES}
    pos_f = positions.astype(F32)[..., None]
    h = x
    for i in range(depth):
        q, k, v, q4, k4, v4, q16, k16, v16, sg, oa = _proj_call(
            h, pos_f, inv_freq, g_in[i][None, :], w_in[i].astype(BF16), avg,
            jnp.tile(q_norm[i], N_HEADS)[None, :], jnp.tile(k_norm[i], N_HEADS)[None, :],
            w_spatial[i], jnp.repeat(b_spatial[i].T, LANES, axis=1),
            ln_v_g[i][None, :], ln_v_b[i][None, :], perms)
        o1, l1 = _attn_call(bias, q[:, None], k[:, None], v[:, None])
        o4, l4 = _attn_call(bias, q4, k4, v4)
        o16, l16 = _attn_call(bias, q16, k16, v16)
        h = _out_call(h, p[i], (o1[:, 0], o4, o16), (l1[:, 0], l4, l16), sg, oa, perms, expand,
                      w_out[i].astype(BF16), g_ple[i][None, :], w_ple_gate[i].astype(BF16),
                      b_ple_gate[i][None, :], w_ple_proj[i].astype(BF16))
    return h
```

```python
import math

import jax
import jax.numpy as jnp
import numpy as np
from jax import lax
from jax.experimental import pallas as pl
from jax.experimental.pallas import tpu as pltpu

D_MODEL = 1024
N_HEADS = 8
HEAD_DIM = 64
ATTN_WIDTH = N_HEADS * HEAD_DIM
DILATIONS = ((128, 1), (512, 4), (2048, 16))
BLK = 128
ROPE_THETA = 10000.0
GMLP_GROUPS = 4
GMLP_WIDTH = D_MODEL // 2
CHUNK = 128
PLE_DIM = 256
EPS = 1e-6
NEG = -1e30
LN2 = math.log(2.0)
LOG2E = math.log2(math.e)

LANES = 128
MXU_DEPTH = 256
N_PAIRS = ATTN_WIDTH // LANES
STAT_REP = LANES // N_HEADS
VMEM_LIMIT = 56 * 1024 * 1024

PROJ_ROWS = 512
ATTN_STEP_ROWS = 2048
ATTN_SUPER = 512
OUT_ROWS = 512
PERM_ROWS = MXU_DEPTH
STRIDES = tuple(d for _, d in DILATIONS if d > 1)

BF16 = jnp.bfloat16
F32 = jnp.float32


def _dot(a, b):
    return jnp.dot(a, b, preferred_element_type=F32)


def _dot_nt(a, b):
    return lax.dot_general(a, b, (((1,), (1,)), ((), ())), preferred_element_type=F32)


def _silu(x):
    return x * jax.nn.sigmoid(x)


def _proj_kernel(x_ref, pos_ref, invf_ref, gin_ref, w_ref, avg_ref, qg_ref, kg_ref, ws_ref, bs_ref,
                 lng_ref, lnb_ref, d4_ref, d16_ref,
                 q_ref, k_ref, v_ref, q4_ref, k4_ref, v4_ref, q16_ref, k16_ref, v16_ref, sg_ref, oa_ref):
    x = x_ref[0]
    n_rows = x.shape[0]
    a = (x * lax.rsqrt(jnp.mean(x * x, axis=-1, keepdims=True) + EPS) * gin_ref[...]).astype(BF16)

    def cols(j):
        return _dot(a, w_ref[:, j * ATTN_WIDTH:(j + 1) * ATTN_WIDTH])

    ang = pos_ref[0] * invf_ref[...]
    cos = jnp.cos(ang)
    sin = jnp.sin(ang)
    lane = lax.broadcasted_iota(jnp.int32, (1, LANES), 1)
    first_half = (lane % HEAD_DIM) < (HEAD_DIM // 2)
    sin_signed = jnp.where(first_half, -sin, sin)

    def norm_rope(z, g_ref, scale):
        ms = _dot((z * z).astype(BF16), avg_ref[...])
        y = z * lax.rsqrt(ms + EPS) * g_ref[...]
        outs = []
        for p in range(N_PAIRS):
            yp = y[:, p * LANES:(p + 1) * LANES]
            swapped = jnp.where(first_half, pltpu.roll(yp, LANES - HEAD_DIM // 2, 1),
                                pltpu.roll(yp, HEAD_DIM // 2, 1))
            outs.append(yp * cos + swapped * sin_signed)
        r = jnp.concatenate(outs, axis=1)
        if scale != 1.0:
            r = r * scale
        return r.astype(BF16)

    def emit(z, nat_ref, regrouped):
        nat_ref[0] = z
        for j in range(n_rows // PERM_ROWS):
            piece = z[j * PERM_ROWS:(j + 1) * PERM_ROWS]
            for (d_ref, out_ref) in regrouped:
                r = out_ref.shape[1]
                n = PERM_ROWS // r
                cm = _dot(d_ref[...], piece).astype(BF16)
                for c in range(r):
                    out_ref[0, c, j * n:(j + 1) * n, :] = cm[c * n:(c + 1) * n]

    emit(norm_rope(cols(0), qg_ref, HEAD_DIM ** -0.5 * LOG2E), q_ref, ((d4_ref, q4_ref), (d16_ref, q16_ref)))
    emit(norm_rope(cols(1), kg_ref, 1.0), k_ref, ((d4_ref, k4_ref), (d16_ref, k16_ref)))
    emit(cols(2).astype(BF16), v_ref, ((d4_ref, v4_ref), (d16_ref, v16_ref)))
    sg_ref[0] = _silu(cols(3)).astype(BF16)

    u = jax.nn.gelu(cols(4))
    gv = jax.nn.gelu(cols(5))
    mu = jnp.mean(gv, axis=-1, keepdims=True)
    gc = gv - mu
    var = jnp.mean(gc * gc, axis=-1, keepdims=True)
    vn = (gc * lax.rsqrt(var + EPS) * lng_ref[...] + lnb_ref[...]).astype(BF16)
    ga = _silu(cols(6))

    r_i = lax.broadcasted_iota(jnp.int32, (CHUNK, CHUNK), 0)
    c_i = lax.broadcasted_iota(jnp.int32, (CHUNK, CHUNK), 1)
    tril = c_i <= r_i
    ws = [jnp.where(tril, ws_ref[g], 0.0).astype(BF16) for g in range(GMLP_GROUPS)]
    bs = bs_ref[...]
    for c in range(n_rows // CHUNK):
        rows = slice(c * CHUNK, (c + 1) * CHUNK)
        sv = jnp.concatenate(
            [_dot(ws[g], vn[rows, g * LANES:(g + 1) * LANES]) for g in range(GMLP_GROUPS)], axis=1) + bs
        oa_ref[0, rows, :] = (u[rows] * sv * ga[rows]).astype(BF16)


def _proj_call(x, pos_f, inv_freq, g_in, w_in, avg, qg, kg, w_s, b_tab, ln_g, ln_b, perms):
    B, S, D = x.shape
    T = PROJ_ROWS
    n_cols = w_in.shape[1]
    W = ATTN_WIDTH
    tok = lambda width: pl.BlockSpec((1, T, width), lambda b, i: (b, i, 0))
    const = lambda shape: pl.BlockSpec(shape, lambda b, i: (0,) * len(shape))
    cls = lambda r: pl.BlockSpec((1, r, T // r, W), lambda b, i: (b, 0, i, 0))
    nat_sds = jax.ShapeDtypeStruct((B, S, W), BF16)
    cls_sds = lambda r: jax.ShapeDtypeStruct((B, r, S // r, W), BF16)
    r4, r16 = STRIDES
    return pl.pallas_call(
        _proj_kernel,
        grid=(B, S // T),
        in_specs=[tok(D), tok(1), const((1, LANES)), const((1, D)), const((D, n_cols)),
                  const((W, W)), const((1, W)), const((1, W)),
                  const((GMLP_GROUPS, CHUNK, CHUNK)), const((CHUNK, GMLP_WIDTH)),
                  const((1, GMLP_WIDTH)), const((1, GMLP_WIDTH)),
                  const((PERM_ROWS, PERM_ROWS)), const((PERM_ROWS, PERM_ROWS))],
        out_specs=[tok(W)] * 3 + [cls(r4)] * 3 + [cls(r16)] * 3 + [tok(W)] * 2,
        out_shape=[nat_sds] * 3 + [cls_sds(r4)] * 3 + [cls_sds(r16)] * 3 + [nat_sds] * 2,
        compiler_params=pltpu.CompilerParams(
            dimension_semantics=("arbitrary", "arbitrary"), vmem_limit_bytes=VMEM_LIMIT),
        name="proj",
    )(x, pos_f, inv_freq, g_in, w_in, avg, qg, kg, w_s, b_tab, ln_g, ln_b, perms[r4][0], perms[r16][0])


def _attn_kernel(bias_ref, q_ref, k_ref, v_ref, kp_ref, vp_ref, o_ref, l_ref, kk_ref, vv_ref):
    n_cls, n_rows = q_ref.shape[0], q_ref.shape[1]
    kk_ref[:, :BLK, :] = kp_ref[...]
    kk_ref[:, BLK:, :] = k_ref[...]
    vv_ref[:, :BLK, :] = vp_ref[...]
    vv_ref[:, BLK:, :] = v_ref[...]

    first_tile = pl.program_id(2) == 0
    lane = lax.broadcasted_iota(jnp.int32, (1, LANES), 1)
    in_a = lane < HEAD_DIM
    group = lane // STAT_REP % N_PAIRS
    ones = jnp.ones((2 * BLK, LANES), BF16)
    per_cls = n_rows // ATTN_SUPER

    def super_block(it, carry):
        cls = it // per_cls
        base = (it % per_cls) * ATTN_SUPER
        for s in range(ATTN_SUPER // BLK):
            row = pl.multiple_of(base + s * BLK, BLK)
            if s == 0:
                bias = bias_ref[jnp.logical_and(first_tile, base == 0).astype(jnp.int32)]
            else:
                bias = bias_ref[0]
            packed = None
            for p in range(N_PAIRS):
                lanes = slice(p * LANES, (p + 1) * LANES)
                qp = q_ref[cls, pl.ds(row, BLK), lanes]
                kk = kk_ref[cls, pl.ds(row, 2 * BLK), lanes]
                vv = vv_ref[cls, pl.ds(row, 2 * BLK), lanes]
                zero = jnp.zeros_like(qp)
                q2 = jnp.concatenate([jnp.where(in_a, qp, zero), jnp.where(in_a, zero, qp)], axis=0)
                sc = _dot_nt(q2, kk) + bias
                m = jnp.max(sc, axis=-1, keepdims=True)
                e = jnp.exp2(sc - m).astype(BF16)
                res = _dot(e, jnp.concatenate([vv, ones], axis=1))
                num = jnp.where(in_a, res[:BLK, :LANES], res[BLK:, :LANES])
                den = jnp.where(in_a, res[:BLK, LANES:], res[BLK:, LANES:])
                o_ref[cls, pl.ds(row, BLK), lanes] = (num / den).astype(o_ref.dtype)
                lse = jnp.where(in_a, m[:BLK], m[BLK:]) * LN2 + jnp.log(den)
                packed = lse if packed is None else jnp.where(group == p, lse, packed)
            l_ref[cls, pl.ds(row, BLK), :] = packed
        return carry

    lax.fori_loop(0, n_cls * per_cls, super_block, 0)


def _attn_call(bias, q, k, v):
    B, r, L, W = q.shape
    R = min(ATTN_STEP_ROWS, L)
    CB = min(r, ATTN_STEP_ROWS // R)
    cur = lambda width: pl.BlockSpec((None, CB, R, width), lambda b, g, i: (b, g, i, 0))
    prev = pl.BlockSpec((None, CB, BLK, W), lambda b, g, i: (b, g, jnp.maximum(i * (R // BLK) - 1, 0), 0))
    return pl.pallas_call(
        _attn_kernel,
        grid=(B, r // CB, L // R),
        in_specs=[pl.BlockSpec(bias.shape, lambda b, g, i: (0, 0, 0)), cur(W), cur(W), cur(W), prev, prev],
        out_specs=[cur(W), cur(LANES)],
        out_shape=[jax.ShapeDtypeStruct((B, r, L, W), BF16), jax.ShapeDtypeStruct((B, r, L, LANES), F32)],
        scratch_shapes=[pltpu.VMEM((CB, R + BLK, W), BF16), pltpu.VMEM((CB, R + BLK, W), BF16)],
        compiler_params=pltpu.CompilerParams(
            dimension_semantics=("arbitrary", "arbitrary", "arbitrary"), vmem_limit_bytes=VMEM_LIMIT),
        name=f"attn_d{r}",
    )(bias, q, k, v, k, v)


def _out_kernel(x_ref, p_ref, o1_ref, o4_ref, o16_ref, l1_ref, l4_ref, l16_ref, sg_ref, oa_ref,
                u4_ref, u16_ref, exp_ref, wo_ref, gple_ref, wg_ref, bg_ref, wp_ref, out_ref):
    n_rows = x_ref.shape[1]

    def to_token_order(ref, u_ref, split):
        r = ref.shape[0]
        n = PERM_ROWS // r
        pieces = []
        for j in range(n_rows // PERM_ROWS):
            cm = jnp.concatenate([ref[c, j * n:(j + 1) * n, :] for c in range(r)], axis=0)
            if split:
                hi = cm.astype(BF16)
                lo = (cm - hi.astype(F32)).astype(BF16)
                t = _dot(u_ref[...], jnp.concatenate([hi, lo], axis=1))
                pieces.append(t[:, :cm.shape[1]] + t[:, cm.shape[1]:])
            else:
                pieces.append(_dot(u_ref[...], cm))
        return jnp.concatenate(pieces, axis=0)

    lses = [l1_ref[0], to_token_order(l4_ref, u4_ref, True), to_token_order(l16_ref, u16_ref, True)]
    outs = [o1_ref[0].astype(F32), to_token_order(o4_ref, u4_ref, False),
            to_token_order(o16_ref, u16_ref, False)]
    m = jnp.maximum(jnp.maximum(lses[0], lses[1]), lses[2])
    es = [jnp.exp(l - m) for l in lses]
    inv = 1.0 / (es[0] + es[1] + es[2])
    ob = None
    for e, o in zip(es, outs):
        w = e * inv
        w_hi = w.astype(BF16)
        w_lo = (w - w_hi.astype(F32)).astype(BF16)
        w_full = _dot(jnp.concatenate([w_hi, w_lo], axis=1), exp_ref[...])
        ob = w_full * o if ob is None else ob + w_full * o
    mb = (ob * sg_ref[0].astype(F32)).astype(BF16)

    h = x_ref[0] + _dot(mb, wo_ref[:ATTN_WIDTH, :]) + _dot(oa_ref[0], wo_ref[ATTN_WIDTH:, :])
    n = (h * lax.rsqrt(jnp.mean(h * h, axis=-1, keepdims=True) + EPS) * gple_ref[...]).astype(BF16)
    gate = jax.nn.sigmoid(_dot(n, wg_ref[...]) + bg_ref[...])
    out_ref[0] = h + gate * _dot(p_ref[0].astype(BF16), wp_ref[...])


def _out_call(x, p, os_, ls_, sg, oa, perms, expand, w_out, g_ple, w_gate, b_gate, w_proj):
    B, S, D = x.shape
    T = OUT_ROWS
    tok = lambda width: pl.BlockSpec((1, T, width), lambda b, i: (b, i, 0))
    cls = lambda r, width: pl.BlockSpec((None, r, T // r, width), lambda b, i: (b, 0, i, 0))
    const = lambda shape: pl.BlockSpec(shape, lambda b, i: (0,) * len(shape))
    r4, r16 = STRIDES
    W = ATTN_WIDTH
    return pl.pallas_call(
        _out_kernel,
        grid=(B, S // T),
        in_specs=[tok(D), tok(PLE_DIM), tok(W), cls(r4, W), cls(r16, W), tok(LANES), cls(r4, LANES),
                  cls(r16, LANES), tok(W), tok(GMLP_WIDTH),
                  const((PERM_ROWS, PERM_ROWS)), const((PERM_ROWS, PERM_ROWS)), const(expand.shape),
                  const(w_out.shape), const((1, D)), const(w_gate.shape), const((1, D)), const(w_proj.shape)],
        out_specs=tok(D),
        out_shape=jax.ShapeDtypeStruct((B, S, D), x.dtype),
        compiler_params=pltpu.CompilerParams(
            dimension_semantics=("arbitrary", "arbitrary"), vmem_limit_bytes=VMEM_LIMIT),
        name="out",
    )(x, p, *os_, *ls_, sg, oa, perms[r4][1], perms[r16][1], expand, w_out, g_ple, w_gate, b_gate, w_proj)


def _band_bias():
    qi = np.arange(BLK)[:, None]
    col = np.arange(2 * BLK)[None, :]
    dist = qi + BLK - col
    band = (dist >= 0) & (dist <= BLK)
    both = np.stack([band, band & (col >= BLK)])
    bias = np.where(both, 0.0, NEG).astype(np.float32)
    return jnp.asarray(np.concatenate([bias, bias], axis=1))


def _head_average():
    h = np.arange(ATTN_WIDTH) // HEAD_DIM
    return jnp.asarray((h[:, None] == h[None, :]).astype(np.float32) / HEAD_DIM, dtype=BF16)


def _stat_expand():
    lane = np.arange(LANES)
    g = lane // STAT_REP
    head = np.where(g < N_PAIRS, 2 * g, 2 * (g - N_PAIRS) + 1)
    pick = (lane % STAT_REP) == 0
    col_head = np.arange(ATTN_WIDTH) // HEAD_DIM
    e = ((head[:, None] == col_head[None, :]) & pick[:, None]).astype(np.float32)
    return jnp.asarray(np.concatenate([e, e], axis=0), dtype=BF16)


def _class_perms(r):
    t = np.arange(PERM_ROWS)
    grouped_pos = (t % r) * (PERM_ROWS // r) + t // r
    d = np.zeros((PERM_ROWS, PERM_ROWS), np.float32)
    d[grouped_pos, t] = 1.0
    return jnp.asarray(d, dtype=BF16), jnp.asarray(d.T, dtype=BF16)


def _inv_freq_row():
    half = HEAD_DIM // 2
    inv = jnp.exp(-math.log(ROPE_THETA) * jnp.arange(half, dtype=F32) / half)
    return jnp.tile(inv, LANES // half)[None, :]


def kernel(x, p, positions, g_in, w_in, q_norm, k_norm, w_spatial, b_spatial, ln_v_g, ln_v_b, w_out,
           g_ple, w_ple_gate, b_ple_gate, w_ple_proj):
    B, S, _ = x.shape
    depth = w_in.shape[0]
    bias = _band_bias()
    avg = _head_average()
    expand = _stat_expand()
    inv_freq = _inv_freq_row()
    perms = {r: _class_perms(r) for r in STRIDES}
    pos_f = positions.astype(F32)[..., None]
    h = x
    for i in range(depth):
        q, k, v, q4, k4, v4, q16, k16, v16, sg, oa = _proj_call(
            h, pos_f, inv_freq, g_in[i][None, :], w_in[i].astype(BF16), avg,
            jnp.tile(q_norm[i], N_HEADS)[None, :], jnp.tile(k_norm[i], N_HEADS)[None, :],
            w_spatial[i], jnp.repeat(b_spatial[i].T, LANES, axis=1),
            ln_v_g[i][None, :], ln_v_b[i][None, :], perms)
        o1, l1 = _attn_call(bias, q[:, None], k[:, None], v[:, None])
        o4, l4 = _attn_call(bias, q4, k4, v4)
        o16, l16 = _attn_call(bias, q16, k16, v16)
        h = _out_call(h, p[i], (o1[:, 0], o4, o16), (l1[:, 0], l4, l16), sg, oa, perms, expand,
                      w_out[i].astype(BF16), g_ple[i][None, :], w_ple_gate[i].astype(BF16),
                      b_ple_gate[i][None, :], w_ple_proj[i].astype(BF16))
    return h
```

```python
import math

import jax
import jax.numpy as jnp
import numpy as np
from jax import lax
from jax.experimental import pallas as pl
from jax.experimental.pallas import tpu as pltpu

D_MODEL = 1024
N_HEADS = 8
HEAD_DIM = 64
ATTN_WIDTH = N_HEADS * HEAD_DIM
DILATIONS = ((128, 1), (512, 4), (2048, 16))
BLK = 128
ROPE_THETA = 10000.0
GMLP_GROUPS = 4
GMLP_WIDTH = D_MODEL // 2
CHUNK = 128
PLE_DIM = 256
EPS = 1e-6
NEG = -1e30
LN2 = math.log(2.0)
LOG2E = math.log2(math.e)

LANES = 128
MXU_DEPTH = 256
N_PAIRS = ATTN_WIDTH // LANES
STAT_REP = LANES // N_HEADS
VMEM_LIMIT = 56 * 1024 * 1024

PROJ_ROWS = 512
ATTN_STEP_ROWS = 2048
ATTN_SUPER = 512
OUT_ROWS = 512
PERM_ROWS = MXU_DEPTH
STRIDES = tuple(d for _, d in DILATIONS if d > 1)

BF16 = jnp.bfloat16
F32 = jnp.float32


def _dot(a, b):
    return jnp.dot(a, b, preferred_element_type=F32)


def _dot_nt(a, b):
    return lax.dot_general(a, b, (((1,), (1,)), ((), ())), preferred_element_type=F32)


def _silu(x):
    return x * jax.nn.sigmoid(x)


def _project_pieces(x_ref, gin_ref, w_ref, z_ref):
    n_rows = x_ref.shape[1]
    half = n_rows // 2
    normed = {}

    def piece(h, j):
        rows = slice(h * half, (h + 1) * half)
        if h not in normed:
            x = x_ref[0, rows, :]
            scale = lax.rsqrt(jnp.mean(x * x, axis=-1, keepdims=True) + EPS)
            normed[h] = (x * scale * gin_ref[...]).astype(BF16)
        cols = slice(j * ATTN_WIDTH, (j + 1) * ATTN_WIDTH)
        z_ref[rows, cols] = _dot(normed[h], w_ref[:, cols])

    return [(lambda h=h, j=j: piece(h, j)) for h in range(2) for j in range(w_ref.shape[1] // ATTN_WIDTH)]


def _finish_pieces(z_ref, pos_ref, invf_ref, avg_ref, qg_ref, kg_ref, ws_ref, bs_ref, lng_ref, lnb_ref,
                   d4_ref, d16_ref, q_ref, k_ref, v_ref, q4_ref, k4_ref, v4_ref, q16_ref, k16_ref, v16_ref,
                   sg_ref, oa_ref):
    n_rows = z_ref.shape[0]
    blocks = [slice(b * CHUNK, (b + 1) * CHUNK) for b in range(n_rows // CHUNK)]
    lane = lax.broadcasted_iota(jnp.int32, (1, LANES), 1)
    first_half = (lane % HEAD_DIM) < (HEAD_DIM // 2)
    cos, sin_signed, vn = {}, {}, {}

    def cols(j):
        return slice(j * ATTN_WIDTH, (j + 1) * ATTN_WIDTH)

    def tables(b):
        ang = pos_ref[0, blocks[b], :] * invf_ref[...]
        cos[b] = jnp.cos(ang)
        sin = jnp.sin(ang)
        sin_signed[b] = jnp.where(first_half, -sin, sin)

    def norm_rope(b, j, g_ref, out_ref, scale):
        z = z_ref[blocks[b], cols(j)]
        ms = _dot((z * z).astype(BF16), avg_ref[...])
        y = z * lax.rsqrt(ms + EPS) * g_ref[...]
        outs = []
        for p in range(N_PAIRS):
            yp = y[:, p * LANES:(p + 1) * LANES]
            swapped = jnp.where(first_half, pltpu.roll(yp, LANES - HEAD_DIM // 2, 1),
                                pltpu.roll(yp, HEAD_DIM // 2, 1))
            outs.append(yp * cos[b] + swapped * sin_signed[b])
        r = jnp.concatenate(outs, axis=1)
        if scale != 1.0:
            r = r * scale
        out_ref[0, blocks[b], :] = r.astype(BF16)

    def value(b):
        v_ref[0, blocks[b], :] = z_ref[blocks[b], cols(2)].astype(BF16)

    def regroup(j, src_ref, regrouped):
        piece = src_ref[0, j * PERM_ROWS:(j + 1) * PERM_ROWS, :]
        for (d_ref, out_ref) in regrouped:
            r = out_ref.shape[1]
            n = PERM_ROWS // r
            cm = _dot(d_ref[...], piece).astype(BF16)
            for c in range(r):
                out_ref[0, c, j * n:(j + 1) * n, :] = cm[c * n:(c + 1) * n]

    def gate(b):
        sg_ref[0, blocks[b], :] = _silu(z_ref[blocks[b], cols(3)]).astype(BF16)

    def gmlp_pre(b):
        gv = jax.nn.gelu(z_ref[blocks[b], cols(5)])
        mu = jnp.mean(gv, axis=-1, keepdims=True)
        gc = gv - mu
        var = jnp.mean(gc * gc, axis=-1, keepdims=True)
        vn[b] = (gc * lax.rsqrt(var + EPS) * lng_ref[...] + lnb_ref[...]).astype(BF16)

    def gmlp_post(b):
        r_i = lax.broadcasted_iota(jnp.int32, (CHUNK, CHUNK), 0)
        c_i = lax.broadcasted_iota(jnp.int32, (CHUNK, CHUNK), 1)
        sv = jnp.concatenate(
            [_dot(jnp.where(c_i <= r_i, ws_ref[g], 0.0).astype(BF16), vn[b][:, g * LANES:(g + 1) * LANES])
             for g in range(GMLP_GROUPS)], axis=1) + bs_ref[...]
        u = jax.nn.gelu(z_ref[blocks[b], cols(4)])
        ga = _silu(z_ref[blocks[b], cols(6)])
        oa_ref[0, blocks[b], :] = (u * sv * ga).astype(BF16)

    nb = range(len(blocks))
    nc = range(n_rows // PERM_ROWS)
    pieces = []
    for b in nb:
        pieces += [(8.0, lambda b=b: tables(b)),
                   (3.0, lambda b=b: norm_rope(b, 0, qg_ref, q_ref, HEAD_DIM ** -0.5 * LOG2E)),
                   (3.0, lambda b=b: norm_rope(b, 1, kg_ref, k_ref, 1.0)),
                   (0.5, lambda b=b: value(b))]
    pieces += [(1.0, lambda b=b: gate(b)) for b in nb]
    pieces += [(2.0, lambda j=j: regroup(j, q_ref, ((d4_ref, q4_ref), (d16_ref, q16_ref)))) for j in nc]
    pieces += [(3.5, lambda b=b: gmlp_pre(b)) for b in nb]
    pieces += [(2.0, lambda j=j: regroup(j, k_ref, ((d4_ref, k4_ref), (d16_ref, k16_ref)))) for j in nc]
    pieces += [(2.0, lambda j=j: regroup(j, v_ref, ((d4_ref, v4_ref), (d16_ref, v16_ref)))) for j in nc]
    pieces += [(4.0, lambda b=b: gmlp_post(b)) for b in nb]
    return pieces


def _interleave(weighted, uniform):
    total = sum(w for w, _ in weighted)
    out, taken, done = [], 0, 0.0
    for w, piece in weighted:
        while taken < len(uniform) and done >= (taken + 0.5) * total / len(uniform):
            out.append(uniform[taken])
            taken += 1
        out.append(piece)
        done += w
    return out + uniform[taken:]


def _proj_kernel(x_ref, pos_ref, invf_ref, gin_ref, w_ref, *rest):
    finish_refs, (z0_ref, z1_ref) = rest[:-2], rest[-2:]
    step = pl.program_id(0)

    @pl.when(step == 0)
    def _():
        z1_ref[...] = jnp.zeros_like(z1_ref)

    def run(z_write_ref, z_read_ref):
        for piece in _interleave(_finish_pieces(z_read_ref, pos_ref, invf_ref, *finish_refs),
                                 _project_pieces(x_ref, gin_ref, w_ref, z_write_ref)):
            piece()

    @pl.when(step % 2 == 0)
    def _():
        run(z0_ref, z1_ref)

    @pl.when(step % 2 == 1)
    def _():
        run(z1_ref, z0_ref)


def _proj_call(x, pos_f, inv_freq, g_in, w_in, avg, qg, kg, w_s, b_tab, ln_g, ln_b, perms):
    B, S, D = x.shape
    T = PROJ_ROWS
    n_cols = w_in.shape[1]
    W = ATTN_WIDTH
    per_seq = S // T
    n_tiles = B * per_seq
    proj_tile = lambda s: jnp.minimum(s, n_tiles - 1)
    done_tile = lambda s: jnp.maximum(s - 1, 0)
    tok_in = pl.BlockSpec((1, T, D), lambda s: (proj_tile(s) // per_seq, proj_tile(s) % per_seq, 0))
    tok = lambda width: pl.BlockSpec((1, T, width), lambda s: (done_tile(s) // per_seq, done_tile(s) % per_seq, 0))
    cls = lambda r: pl.BlockSpec((1, r, T // r, W),
                                 lambda s: (done_tile(s) // per_seq, 0, done_tile(s) % per_seq, 0))
    const = lambda shape: pl.BlockSpec(shape, lambda s: (0,) * len(shape))
    nat_sds = jax.ShapeDtypeStruct((B, S, W), BF16)
    cls_sds = lambda r: jax.ShapeDtypeStruct((B, r, S // r, W), BF16)
    r4, r16 = STRIDES
    return pl.pallas_call(
        _proj_kernel,
        grid=(n_tiles + 1,),
        scratch_shapes=[pltpu.VMEM((T, n_cols), F32), pltpu.VMEM((T, n_cols), F32)],
        in_specs=[tok_in, tok(1), const((1, LANES)), const((1, D)), const((D, n_cols)),
                  const((W, W)), const((1, W)), const((1, W)),
                  const((GMLP_GROUPS, CHUNK, CHUNK)), const((CHUNK, GMLP_WIDTH)),
                  const((1, GMLP_WIDTH)), const((1, GMLP_WIDTH)),
                  const((PERM_ROWS, PERM_ROWS)), const((PERM_ROWS, PERM_ROWS))],
        out_specs=[tok(W)] * 3 + [cls(r4)] * 3 + [cls(r16)] * 3 + [tok(W)] * 2,
        out_shape=[nat_sds] * 3 + [cls_sds(r4)] * 3 + [cls_sds(r16)] * 3 + [nat_sds] * 2,
        compiler_params=pltpu.CompilerParams(dimension_semantics=("arbitrary",), vmem_limit_bytes=VMEM_LIMIT),
        name="proj",
    )(x, pos_f, inv_freq, g_in, w_in, avg, qg, kg, w_s, b_tab, ln_g, ln_b, perms[r4][0], perms[r16][0])


def _attn_kernel(bias_ref, q_ref, k_ref, v_ref, kp_ref, vp_ref, o_ref, l_ref, kk_ref, vv_ref):
    n_cls, n_rows = q_ref.shape[0], q_ref.shape[1]
    kk_ref[:, :BLK, :] = kp_ref[...]
    kk_ref[:, BLK:, :] = k_ref[...]
    vv_ref[:, :BLK, :] = vp_ref[...]
    vv_ref[:, BLK:, :] = v_ref[...]

    first_tile = pl.program_id(2) == 0
    lane = lax.broadcasted_iota(jnp.int32, (1, LANES), 1)
    in_a = lane < HEAD_DIM
    group = lane // STAT_REP % N_PAIRS
    ones = jnp.ones((2 * BLK, LANES), BF16)
    per_cls = n_rows // ATTN_SUPER

    def super_block(it, carry):
        cls = it // per_cls
        base = (it % per_cls) * ATTN_SUPER
        for s in range(ATTN_SUPER // BLK):
            row = pl.multiple_of(base + s * BLK, BLK)
            if s == 0:
                bias = bias_ref[jnp.logical_and(first_tile, base == 0).astype(jnp.int32)]
            else:
                bias = bias_ref[0]
            packed = None
            for p in range(N_PAIRS):
                lanes = slice(p * LANES, (p + 1) * LANES)
                qp = q_ref[cls, pl.ds(row, BLK), lanes]
                kk = kk_ref[cls, pl.ds(row, 2 * BLK), lanes]
                vv = vv_ref[cls, pl.ds(row, 2 * BLK), lanes]
                zero = jnp.zeros_like(qp)
                q2 = jnp.concatenate([jnp.where(in_a, qp, zero), jnp.where(in_a, zero, qp)], axis=0)
                sc = _dot_nt(q2, kk) + bias
                m = jnp.max(sc, axis=-1, keepdims=True)
                e = jnp.exp2(sc - m).astype(BF16)
                res = _dot(e, jnp.concatenate([vv, ones], axis=1))
                num = jnp.where(in_a, res[:BLK, :LANES], res[BLK:, :LANES])
                den = jnp.where(in_a, res[:BLK, LANES:], res[BLK:, LANES:])
                o_ref[cls, pl.ds(row, BLK), lanes] = (num / den).astype(o_ref.dtype)
                lse = jnp.where(in_a, m[:BLK], m[BLK:]) * LN2 + jnp.log(den)
                packed = lse if packed is None else jnp.where(group == p, lse, packed)
            l_ref[cls, pl.ds(row, BLK), :] = packed
        return carry

    lax.fori_loop(0, n_cls * per_cls, super_block, 0)


def _attn_call(bias, q, k, v):
    B, r, L, W = q.shape
    R = min(ATTN_STEP_ROWS, L)
    CB = min(r, ATTN_STEP_ROWS // R)
    cur = lambda width: pl.BlockSpec((None, CB, R, width), lambda b, g, i: (b, g, i, 0))
    prev = pl.BlockSpec((None, CB, BLK, W), lambda b, g, i: (b, g, jnp.maximum(i * (R // BLK) - 1, 0), 0))
    return pl.pallas_call(
        _attn_kernel,
        grid=(B, r // CB, L // R),
        in_specs=[pl.BlockSpec(bias.shape, lambda b, g, i: (0, 0, 0)), cur(W), cur(W), cur(W), prev, prev],
        out_specs=[cur(W), cur(LANES)],
        out_shape=[jax.ShapeDtypeStruct((B, r, L, W), BF16), jax.ShapeDtypeStruct((B, r, L, LANES), F32)],
        scratch_shapes=[pltpu.VMEM((CB, R + BLK, W), BF16), pltpu.VMEM((CB, R + BLK, W), BF16)],
        compiler_params=pltpu.CompilerParams(
            dimension_semantics=("arbitrary", "arbitrary", "arbitrary"), vmem_limit_bytes=VMEM_LIMIT),
        name=f"attn_d{r}",
    )(bias, q, k, v, k, v)


def _out_kernel(x_ref, p_ref, o1_ref, o4_ref, o16_ref, l1_ref, l4_ref, l16_ref, sg_ref, oa_ref,
                u4_ref, u16_ref, exp_ref, wo_ref, gple_ref, wg_ref, bg_ref, wp_ref, out_ref):
    n_rows = x_ref.shape[1]

    def to_token_order(ref, u_ref, split):
        r = ref.shape[0]
        n = PERM_ROWS // r
        pieces = []
        for j in range(n_rows // PERM_ROWS):
            cm = jnp.concatenate([ref[c, j * n:(j + 1) * n, :] for c in range(r)], axis=0)
            if split:
                hi = cm.astype(BF16)
                lo = (cm - hi.astype(F32)).astype(BF16)
                t = _dot(u_ref[...], jnp.concatenate([hi, lo], axis=1))
                pieces.append(t[:, :cm.shape[1]] + t[:, cm.shape[1]:])
            else:
                pieces.append(_dot(u_ref[...], cm))
        return jnp.concatenate(pieces, axis=0)

    lses = [l1_ref[0], to_token_order(l4_ref, u4_ref, True), to_token_order(l16_ref, u16_ref, True)]
    outs = [o1_ref[0].astype(F32), to_token_order(o4_ref, u4_ref, False),
            to_token_order(o16_ref, u16_ref, False)]
    m = jnp.maximum(jnp.maximum(lses[0], lses[1]), lses[2])
    es = [jnp.exp(l - m) for l in lses]
    inv = 1.0 / (es[0] + es[1] + es[2])
    ob = None
    for e, o in zip(es, outs):
        w = e * inv
        w_hi = w.astype(BF16)
        w_lo = (w - w_hi.astype(F32)).astype(BF16)
        w_full = _dot(jnp.concatenate([w_hi, w_lo], axis=1), exp_ref[...])
        ob = w_full * o if ob is None else ob + w_full * o
    mb = (ob * sg_ref[0].astype(F32)).astype(BF16)

    h = x_ref[0] + _dot(mb, wo_ref[:ATTN_WIDTH, :]) + _dot(oa_ref[0], wo_ref[ATTN_WIDTH:, :])
    n = (h * lax.rsqrt(jnp.mean(h * h, axis=-1, keepdims=True) + EPS) * gple_ref[...]).astype(BF16)
    gate = jax.nn.sigmoid(_dot(n, wg_ref[...]) + bg_ref[...])
    out_ref[0] = h + gate * _dot(p_ref[0].astype(BF16), wp_ref[...])


def _out_call(x, p, os_, ls_, sg, oa, perms, expand, w_out, g_ple, w_gate, b_gate, w_proj):
    B, S, D = x.shape
    T = OUT_ROWS
    tok = lambda width: pl.BlockSpec((1, T, width), lambda b, i: (b, i, 0))
    cls = lambda r, width: pl.BlockSpec((None, r, T // r, width), lambda b, i: (b, 0, i, 0))
    const = lambda shape: pl.BlockSpec(shape, lambda b, i: (0,) * len(shape))
    r4, r16 = STRIDES
    W = ATTN_WIDTH
    return pl.pallas_call(
        _out_kernel,
        grid=(B, S // T),
        in_specs=[tok(D), tok(PLE_DIM), tok(W), cls(r4, W), cls(r16, W), tok(LANES), cls(r4, LANES),
                  cls(r16, LANES), tok(W), tok(GMLP_WIDTH),
                  const((PERM_ROWS, PERM_ROWS)), const((PERM_ROWS, PERM_ROWS)), const(expand.shape),
                  const(w_out.shape), const((1, D)), const(w_gate.shape), const((1, D)), const(w_proj.shape)],
        out_specs=tok(D),
        out_shape=jax.ShapeDtypeStruct((B, S, D), x.dtype),
        compiler_params=pltpu.CompilerParams(
            dimension_semantics=("arbitrary", "arbitrary"), vmem_limit_bytes=VMEM_LIMIT),
        name="out",
    )(x, p, *os_, *ls_, sg, oa, perms[r4][1], perms[r16][1], expand, w_out, g_ple, w_gate, b_gate, w_proj)


def _band_bias():
    qi = np.arange(BLK)[:, None]
    col = np.arange(2 * BLK)[None, :]
    dist = qi + BLK - col
    band = (dist >= 0) & (dist <= BLK)
    both = np.stack([band, band & (col >= BLK)])
    bias = np.where(both, 0.0, NEG).astype(np.float32)
    return jnp.asarray(np.concatenate([bias, bias], axis=1))


def _head_average():
    h = np.arange(ATTN_WIDTH) // HEAD_DIM
    return jnp.asarray((h[:, None] == h[None, :]).astype(np.float32) / HEAD_DIM, dtype=BF16)


def _stat_expand():
    lane = np.arange(LANES)
    g = lane // STAT_REP
    head = np.where(g < N_PAIRS, 2 * g, 2 * (g - N_PAIRS) + 1)
    pick = (lane % STAT_REP) == 0
    col_head = np.arange(ATTN_WIDTH) // HEAD_DIM
    e = ((head[:, None] == col_head[None, :]) & pick[:, None]).astype(np.float32)
    return jnp.asarray(np.concatenate([e, e], axis=0), dtype=BF16)


def _class_perms(r):
    t = np.arange(PERM_ROWS)
    grouped_pos = (t % r) * (PERM_ROWS // r) + t // r
    d = np.zeros((PERM_ROWS, PERM_ROWS), np.float32)
    d[grouped_pos, t] = 1.0
    return jnp.asarray(d, dtype=BF16), jnp.asarray(d.T, dtype=BF16)


def _inv_freq_row():
    half = HEAD_DIM // 2
    inv = jnp.exp(-math.log(ROPE_THETA) * jnp.arange(half, dtype=F32) / half)
    return jnp.tile(inv, LANES // half)[None, :]


def kernel(x, p, positions, g_in, w_in, q_norm, k_norm, w_spatial, b_spatial, ln_v_g, ln_v_b, w_out,
           g_ple, w_ple_gate, b_ple_gate, w_ple_proj):
    B, S, _ = x.shape
    depth = w_in.shape[0]
    bias = _band_bias()
    avg = _head_average()
    expand = _stat_expand()
    inv_freq = _inv_freq_row()
    perms = {r: _class_perms(r) for r in STRIDES}
    pos_f = positions.astype(F32)[..., None]
    h = x
    for i in range(depth):
        q, k, v, q4, k4, v4, q16, k16, v16, sg, oa = _proj_call(
            h, pos_f, inv_freq, g_in[i][None, :], w_in[i].astype(BF16), avg,
            jnp.tile(q_norm[i], N_HEADS)[None, :], jnp.tile(k_norm[i], N_HEADS)[None, :],
            w_spatial[i], jnp.repeat(b_spatial[i].T, LANES, axis=1),
            ln_v_g[i][None, :], ln_v_b[i][None, :], perms)
        o1, l1 = _attn_call(bias, q[:, None], k[:, None], v[:, None])
        o4, l4 = _attn_call(bias, q4, k4, v4)
        o16, l16 = _attn_call(bias, q16, k16, v16)
        h = _out_call(h, p[i], (o1[:, 0], o4, o16), (l1[:, 0], l4, l16), sg, oa, perms, expand,
                      w_out[i].astype(BF16), g_ple[i][None, :], w_ple_gate[i].astype(BF16),
                      b_ple_gate[i][None, :], w_ple_proj[i].astype(BF16))
    return h
```

```python
import math

import jax
import jax.numpy as jnp
import numpy as np
from jax import lax
from jax.experimental import pallas as pl
from jax.experimental.pallas import tpu as pltpu

D_MODEL = 1024
N_HEADS = 8
HEAD_DIM = 64
ATTN_WIDTH = N_HEADS * HEAD_DIM
DILATIONS = ((128, 1), (512, 4), (2048, 16))
BLK = 128
ROPE_THETA = 10000.0
GMLP_GROUPS = 4
GMLP_WIDTH = D_MODEL // 2
CHUNK = 128
PLE_DIM = 256
EPS = 1e-6
NEG = -1e30
LN2 = math.log(2.0)
LOG2E = math.log2(math.e)

LANES = 128
MXU_DEPTH = 256
N_PAIRS = ATTN_WIDTH // LANES
STAT_REP = LANES // N_HEADS
VMEM_LIMIT = 56 * 1024 * 1024

PROJ_ROWS = 512
ATTN_STEP_ROWS = 2048
ATTN_SUPER = 512
OUT_ROWS = 1024
PERM_ROWS = MXU_DEPTH
STRIDES = tuple(d for _, d in DILATIONS if d > 1)

BF16 = jnp.bfloat16
F32 = jnp.float32


def _dot(a, b):
    return jnp.dot(a, b, preferred_element_type=F32)


def _dot_nt(a, b):
    return lax.dot_general(a, b, (((1,), (1,)), ((), ())), preferred_element_type=F32)


def _silu(x):
    return x * jax.nn.sigmoid(x)


def _project_pieces(x_ref, gin_ref, w_ref, z_ref):
    n_rows = x_ref.shape[1]
    half = n_rows // 2
    normed = {}

    def piece(h, j):
        rows = slice(h * half, (h + 1) * half)
        if h not in normed:
            x = x_ref[0, rows, :]
            scale = lax.rsqrt(jnp.mean(x * x, axis=-1, keepdims=True) + EPS)
            normed[h] = (x * scale * gin_ref[...]).astype(BF16)
        cols = slice(j * ATTN_WIDTH, (j + 1) * ATTN_WIDTH)
        z_ref[rows, cols] = _dot(normed[h], w_ref[:, cols])

    return [(lambda h=h, j=j: piece(h, j)) for h in range(2) for j in range(w_ref.shape[1] // ATTN_WIDTH)]


def _finish_pieces(z_ref, pos_ref, invf_ref, avg_ref, qg_ref, kg_ref, ws_ref, bs_ref, lng_ref, lnb_ref,
                   d4_ref, d16_ref, q_ref, k_ref, v_ref, q4_ref, k4_ref, v4_ref, q16_ref, k16_ref, v16_ref,
                   sg_ref, oa_ref):
    n_rows = z_ref.shape[0]
    blocks = [slice(b * CHUNK, (b + 1) * CHUNK) for b in range(n_rows // CHUNK)]
    lane = lax.broadcasted_iota(jnp.int32, (1, LANES), 1)
    first_half = (lane % HEAD_DIM) < (HEAD_DIM // 2)
    cos, sin_signed, vn = {}, {}, {}

    def cols(j):
        return slice(j * ATTN_WIDTH, (j + 1) * ATTN_WIDTH)

    def tables(b):
        ang = pos_ref[0, blocks[b], :] * invf_ref[...]
        cos[b] = jnp.cos(ang)
        sin = jnp.sin(ang)
        sin_signed[b] = jnp.where(first_half, -sin, sin)

    def norm_rope(b, j, g_ref, out_ref, scale):
        z = z_ref[blocks[b], cols(j)]
        sq = (z * z).astype(BF16)
        ms = jnp.concatenate([_dot(sq[:, c:c + MXU_DEPTH], avg_ref[c:c + MXU_DEPTH, c:c + MXU_DEPTH])
                              for c in range(0, ATTN_WIDTH, MXU_DEPTH)], axis=1)
        y = z * lax.rsqrt(ms + EPS) * g_ref[...]
        outs = []
        for p in range(N_PAIRS):
            yp = y[:, p * LANES:(p + 1) * LANES]
            swapped = jnp.where(first_half, pltpu.roll(yp, LANES - HEAD_DIM // 2, 1),
                                pltpu.roll(yp, HEAD_DIM // 2, 1))
            outs.append(yp * cos[b] + swapped * sin_signed[b])
        r = jnp.concatenate(outs, axis=1)
        if scale != 1.0:
            r = r * scale
        out_ref[0, blocks[b], :] = r.astype(BF16)

    def value(b):
        v_ref[0, blocks[b], :] = z_ref[blocks[b], cols(2)].astype(BF16)

    def regroup(j, src_ref, regrouped):
        piece = src_ref[0, j * PERM_ROWS:(j + 1) * PERM_ROWS, :]
        for (d_ref, out_ref) in regrouped:
            r = out_ref.shape[1]
            n = PERM_ROWS // r
            cm = _dot(d_ref[...], piece).astype(BF16)
            for c in range(r):
                out_ref[0, c, j * n:(j + 1) * n, :] = cm[c * n:(c + 1) * n]

    def gate(b):
        sg_ref[0, blocks[b], :] = _silu(z_ref[blocks[b], cols(3)]).astype(BF16)

    def gmlp_pre(b):
        gv = jax.nn.gelu(z_ref[blocks[b], cols(5)])
        mu = jnp.mean(gv, axis=-1, keepdims=True)
        gc = gv - mu
        var = jnp.mean(gc * gc, axis=-1, keepdims=True)
        vn[b] = (gc * lax.rsqrt(var + EPS) * lng_ref[...] + lnb_ref[...]).astype(BF16)

    def gmlp_post(b):
        r_i = lax.broadcasted_iota(jnp.int32, (CHUNK, CHUNK), 0)
        c_i = lax.broadcasted_iota(jnp.int32, (CHUNK, CHUNK), 1)
        sv = jnp.concatenate(
            [_dot(jnp.where(c_i <= r_i, ws_ref[g], 0.0).astype(BF16), vn[b][:, g * LANES:(g + 1) * LANES])
             for g in range(GMLP_GROUPS)], axis=1) + bs_ref[...]
        u = jax.nn.gelu(z_ref[blocks[b], cols(4)])
        ga = _silu(z_ref[blocks[b], cols(6)])
        oa_ref[0, blocks[b], :] = (u * sv * ga).astype(BF16)

    nb = range(len(blocks))
    nc = range(n_rows // PERM_ROWS)
    pieces = []
    for b in nb:
        pieces += [(8.0, lambda b=b: tables(b)),
                   (3.0, lambda b=b: norm_rope(b, 0, qg_ref, q_ref, HEAD_DIM ** -0.5 * LOG2E)),
                   (3.0, lambda b=b: norm_rope(b, 1, kg_ref, k_ref, 1.0)),
                   (0.5, lambda b=b: value(b))]
    pieces += [(1.0, lambda b=b: gate(b)) for b in nb]
    pieces += [(2.0, lambda j=j: regroup(j, q_ref, ((d4_ref, q4_ref), (d16_ref, q16_ref)))) for j in nc]
    pieces += [(3.5, lambda b=b: gmlp_pre(b)) for b in nb]
    pieces += [(2.0, lambda j=j: regroup(j, k_ref, ((d4_ref, k4_ref), (d16_ref, k16_ref)))) for j in nc]
    pieces += [(2.0, lambda j=j: regroup(j, v_ref, ((d4_ref, v4_ref), (d16_ref, v16_ref)))) for j in nc]
    pieces += [(4.0, lambda b=b: gmlp_post(b)) for b in nb]
    return pieces


def _interleave(weighted, uniform):
    total = sum(w for w, _ in weighted)
    out, taken, done = [], 0, 0.0
    for w, piece in weighted:
        while taken < len(uniform) and done >= (taken + 0.5) * total / len(uniform):
            out.append(uniform[taken])
            taken += 1
        out.append(piece)
        done += w
    return out + uniform[taken:]


def _proj_kernel(x_ref, pos_ref, invf_ref, gin_ref, w_ref, *rest):
    finish_refs, (z0_ref, z1_ref) = rest[:-2], rest[-2:]
    step = pl.program_id(0)

    @pl.when(step == 0)
    def _():
        z1_ref[...] = jnp.zeros_like(z1_ref)

    def run(z_write_ref, z_read_ref):
        for piece in _interleave(_finish_pieces(z_read_ref, pos_ref, invf_ref, *finish_refs),
                                 _project_pieces(x_ref, gin_ref, w_ref, z_write_ref)):
            piece()

    @pl.when(step % 2 == 0)
    def _():
        run(z0_ref, z1_ref)

    @pl.when(step % 2 == 1)
    def _():
        run(z1_ref, z0_ref)


def _proj_call(x, pos_f, inv_freq, g_in, w_in, avg, qg, kg, w_s, b_tab, ln_g, ln_b, perms):
    B, S, D = x.shape
    T = PROJ_ROWS
    n_cols = w_in.shape[1]
    W = ATTN_WIDTH
    per_seq = S // T
    n_tiles = B * per_seq
    proj_tile = lambda s: jnp.minimum(s, n_tiles - 1)
    done_tile = lambda s: jnp.maximum(s - 1, 0)
    tok_in = pl.BlockSpec((1, T, D), lambda s: (proj_tile(s) // per_seq, proj_tile(s) % per_seq, 0))
    tok = lambda width: pl.BlockSpec((1, T, width), lambda s: (done_tile(s) // per_seq, done_tile(s) % per_seq, 0))
    cls = lambda r: pl.BlockSpec((1, r, T // r, W),
                                 lambda s: (done_tile(s) // per_seq, 0, done_tile(s) % per_seq, 0))
    const = lambda shape: pl.BlockSpec(shape, lambda s: (0,) * len(shape))
    nat_sds = jax.ShapeDtypeStruct((B, S, W), BF16)
    cls_sds = lambda r: jax.ShapeDtypeStruct((B, r, S // r, W), BF16)
    r4, r16 = STRIDES
    return pl.pallas_call(
        _proj_kernel,
        grid=(n_tiles + 1,),
        scratch_shapes=[pltpu.VMEM((T, n_cols), F32), pltpu.VMEM((T, n_cols), F32)],
        in_specs=[tok_in, tok(1), const((1, LANES)), const((1, D)), const((D, n_cols)),
                  const((W, W)), const((1, W)), const((1, W)),
                  const((GMLP_GROUPS, CHUNK, CHUNK)), const((CHUNK, GMLP_WIDTH)),
                  const((1, GMLP_WIDTH)), const((1, GMLP_WIDTH)),
                  const((PERM_ROWS, PERM_ROWS)), const((PERM_ROWS, PERM_ROWS))],
        out_specs=[tok(W)] * 3 + [cls(r4)] * 3 + [cls(r16)] * 3 + [tok(W)] * 2,
        out_shape=[nat_sds] * 3 + [cls_sds(r4)] * 3 + [cls_sds(r16)] * 3 + [nat_sds] * 2,
        compiler_params=pltpu.CompilerParams(dimension_semantics=("arbitrary",), vmem_limit_bytes=VMEM_LIMIT),
        name="proj",
    )(x, pos_f, inv_freq, g_in, w_in, avg, qg, kg, w_s, b_tab, ln_g, ln_b, perms[r4][0], perms[r16][0])


def _attn_kernel(bias_ref, q_ref, k_ref, v_ref, kp_ref, vp_ref, o_ref, l_ref, kk_ref, vv_ref):
    n_cls, n_rows = q_ref.shape[0], q_ref.shape[1]
    kk_ref[:, :BLK, :] = kp_ref[...]
    kk_ref[:, BLK:, :] = k_ref[...]
    vv_ref[:, :BLK, :] = vp_ref[...]
    vv_ref[:, BLK:, :] = v_ref[...]

    first_tile = pl.program_id(2) == 0
    lane = lax.broadcasted_iota(jnp.int32, (1, LANES), 1)
    in_a = lane < HEAD_DIM
    group = lane // STAT_REP % N_PAIRS
    ones = jnp.ones((2 * BLK, LANES), BF16)
    per_cls = n_rows // ATTN_SUPER

    def super_block(it, carry):
        cls = it // per_cls
        base = (it % per_cls) * ATTN_SUPER
        for s in range(ATTN_SUPER // BLK):
            row = pl.multiple_of(base + s * BLK, BLK)
            if s == 0:
                bias = bias_ref[jnp.logical_and(first_tile, base == 0).astype(jnp.int32)]
            else:
                bias = bias_ref[0]
            packed = None
            for p in range(N_PAIRS):
                lanes = slice(p * LANES, (p + 1) * LANES)
                qp = q_ref[cls, pl.ds(row, BLK), lanes]
                kk = kk_ref[cls, pl.ds(row, 2 * BLK), lanes]
                vv = vv_ref[cls, pl.ds(row, 2 * BLK), lanes]
                zero = jnp.zeros_like(qp)
                q2 = jnp.concatenate([jnp.where(in_a, qp, zero), jnp.where(in_a, zero, qp)], axis=0)
                sc = _dot_nt(q2, kk) + bias
                m = jnp.max(sc, axis=-1, keepdims=True)
                e = jnp.exp2(sc - m).astype(BF16)
                res = _dot(e, jnp.concatenate([vv, ones], axis=1))
                num = jnp.where(in_a, res[:BLK, :LANES], res[BLK:, :LANES])
                den = jnp.where(in_a, res[:BLK, LANES:], res[BLK:, LANES:])
                o_ref[cls, pl.ds(row, BLK), lanes] = (num / den).astype(o_ref.dtype)
                lse = jnp.where(in_a, m[:BLK], m[BLK:]) * LN2 + jnp.log(den)
                packed = lse if packed is None else jnp.where(group == p, lse, packed)
            l_ref[cls, pl.ds(row, BLK), :] = packed
        return carry

    lax.fori_loop(0, n_cls * per_cls, super_block, 0)


def _attn_call(bias, q, k, v):
    B, r, L, W = q.shape
    R = min(ATTN_STEP_ROWS, L)
    CB = min(r, ATTN_STEP_ROWS // R)
    cur = lambda width: pl.BlockSpec((None, CB, R, width), lambda b, g, i: (b, g, i, 0))
    prev = pl.BlockSpec((None, CB, BLK, W), lambda b, g, i: (b, g, jnp.maximum(i * (R // BLK) - 1, 0), 0))
    return pl.pallas_call(
        _attn_kernel,
        grid=(B, r // CB, L // R),
        in_specs=[pl.BlockSpec(bias.shape, lambda b, g, i: (0, 0, 0)), cur(W), cur(W), cur(W), prev, prev],
        out_specs=[cur(W), cur(LANES)],
        out_shape=[jax.ShapeDtypeStruct((B, r, L, W), BF16), jax.ShapeDtypeStruct((B, r, L, LANES), F32)],
        scratch_shapes=[pltpu.VMEM((CB, R + BLK, W), BF16), pltpu.VMEM((CB, R + BLK, W), BF16)],
        compiler_params=pltpu.CompilerParams(
            dimension_semantics=("arbitrary", "arbitrary", "arbitrary"), vmem_limit_bytes=VMEM_LIMIT),
        name=f"attn_d{r}",
    )(bias, q, k, v, k, v)


def _out_kernel(x_ref, p_ref, o1_ref, o4_ref, o16_ref, l1_ref, l4_ref, l16_ref, sg_ref, oa_ref,
                u4_ref, u16_ref, exp_ref, wo_ref, gple_ref, wg_ref, bg_ref, wp_ref, out_ref):
    n_blocks = x_ref.shape[1] // PERM_ROWS
    st = [dict() for _ in range(n_blocks)]

    def rows(b):
        return slice(b * PERM_ROWS, (b + 1) * PERM_ROWS)

    def grouped(ref, b):
        r = ref.shape[0]
        n = PERM_ROWS // r
        return jnp.concatenate([ref[c, b * n:(b + 1) * n, :] for c in range(r)], axis=0)

    def to_token_order(b):
        s = st[b]
        s["o4"] = _dot(u4_ref[...], grouped(o4_ref, b))
        s["o16"] = _dot(u16_ref[...], grouped(o16_ref, b))
        for name, ref, u_ref in (("l4", l4_ref, u4_ref), ("l16", l16_ref, u16_ref)):
            cm = grouped(ref, b)
            hi = cm.astype(BF16)
            lo = (cm - hi.astype(F32)).astype(BF16)
            t = _dot(u_ref[...], jnp.concatenate([hi, lo], axis=1))
            s[name] = t[:, :LANES] + t[:, LANES:]

    def merge_weights(b):
        s = st[b]
        l1 = l1_ref[0, rows(b), :]
        m = jnp.maximum(jnp.maximum(l1, s["l4"]), s["l16"])
        e1, e4, e16 = jnp.exp(l1 - m), jnp.exp(s["l4"] - m), jnp.exp(s["l16"] - m)
        inv = 1.0 / (e1 + e4 + e16)
        s["w"] = []
        for e in (e1, e4):
            w = e * inv
            w_hi = w.astype(BF16)
            w_lo = (w - w_hi.astype(F32)).astype(BF16)
            s["w"].append(jnp.concatenate([w_hi, w_lo], axis=1))

    def expand_weights(b):
        s = st[b]
        s["wf"] = [_dot(w, exp_ref[...]) for w in s["w"]]

    def merge(b):
        s = st[b]
        o1 = o1_ref[0, rows(b), :].astype(F32)
        ob = s["o16"] + s["wf"][0] * (o1 - s["o16"]) + s["wf"][1] * (s["o4"] - s["o16"])
        s["mb"] = (ob * sg_ref[0, rows(b), :].astype(F32)).astype(BF16)

    def project(b):
        s = st[b]
        s["t"] = _dot(s["mb"], wo_ref[:ATTN_WIDTH, :]) + _dot(oa_ref[0, rows(b), :], wo_ref[ATTN_WIDTH:, :])

    def residual_norm(b):
        s = st[b]
        h = x_ref[0, rows(b), :] + s["t"]
        s["h"] = h
        s["n"] = (h * lax.rsqrt(jnp.mean(h * h, axis=-1, keepdims=True) + EPS) * gple_ref[...]).astype(BF16)

    def embed_dots(b):
        s = st[b]
        s["g"] = _dot(s["n"], wg_ref[...])
        s["pp"] = _dot(p_ref[0, rows(b), :].astype(BF16), wp_ref[...])

    def embed_out(b):
        s = st[b]
        out_ref[0, rows(b), :] = s["h"] + jax.nn.sigmoid(s["g"] + bg_ref[...]) * s["pp"]

    stages = (to_token_order, merge_weights, expand_weights, merge, project, residual_norm, embed_dots, embed_out)
    for t in range(len(stages) + n_blocks - 1):
        for b in range(n_blocks):
            if 0 <= t - b < len(stages):
                stages[t - b](b)


def _out_call(x, p, os_, ls_, sg, oa, perms, expand, w_out, g_ple, w_gate, b_gate, w_proj):
    B, S, D = x.shape
    T = OUT_ROWS
    tok = lambda width: pl.BlockSpec((1, T, width), lambda b, i: (b, i, 0))
    cls = lambda r, width: pl.BlockSpec((None, r, T // r, width), lambda b, i: (b, 0, i, 0))
    const = lambda shape: pl.BlockSpec(shape, lambda b, i: (0,) * len(shape))
    r4, r16 = STRIDES
    W = ATTN_WIDTH
    return pl.pallas_call(
        _out_kernel,
        grid=(B, S // T),
        in_specs=[tok(D), tok(PLE_DIM), tok(W), cls(r4, W), cls(r16, W), tok(LANES), cls(r4, LANES),
                  cls(r16, LANES), tok(W), tok(GMLP_WIDTH),
                  const((PERM_ROWS, PERM_ROWS)), const((PERM_ROWS, PERM_ROWS)), const(expand.shape),
                  const(w_out.shape), const((1, D)), const(w_gate.shape), const((1, D)), const(w_proj.shape)],
        out_specs=tok(D),
        out_shape=jax.ShapeDtypeStruct((B, S, D), x.dtype),
        compiler_params=pltpu.CompilerParams(
            dimension_semantics=("arbitrary", "arbitrary"), vmem_limit_bytes=VMEM_LIMIT),
        name="out",
    )(x, p, *os_, *ls_, sg, oa, perms[r4][1], perms[r16][1], expand, w_out, g_ple, w_gate, b_gate, w_proj)


def _band_bias():
    qi = np.arange(BLK)[:, None]
    col = np.arange(2 * BLK)[None, :]
    dist = qi + BLK - col
    band = (dist >= 0) & (dist <= BLK)
    both = np.stack([band, band & (col >= BLK)])
    bias = np.where(both, 0.0, NEG).astype(np.float32)
    return jnp.asarray(np.concatenate([bias, bias], axis=1))


def _head_average():
    h = np.arange(ATTN_WIDTH) // HEAD_DIM
    return jnp.asarray((h[:, None] == h[None, :]).astype(np.float32) / HEAD_DIM, dtype=BF16)


def _stat_expand():
    lane = np.arange(LANES)
    g = lane // STAT_REP
    head = np.where(g < N_PAIRS, 2 * g, 2 * (g - N_PAIRS) + 1)
    pick = (lane % STAT_REP) == 0
    col_head = np.arange(ATTN_WIDTH) // HEAD_DIM
    e = ((head[:, None] == col_head[None, :]) & pick[:, None]).astype(np.float32)
    return jnp.asarray(np.concatenate([e, e], axis=0), dtype=BF16)


def _class_perms(r):
    t = np.arange(PERM_ROWS)
    grouped_pos = (t % r) * (PERM_ROWS // r) + t // r
    d = np.zeros((PERM_ROWS, PERM_ROWS), np.float32)
    d[grouped_pos, t] = 1.0
    return jnp.asarray(d, dtype=BF16), jnp.asarray(d.T, dtype=BF16)


def _inv_freq_row():
    half = HEAD_DIM // 2
    inv = jnp.exp(-math.log(ROPE_THETA) * jnp.arange(half, dtype=F32) / half)
    return jnp.tile(inv, LANES // half)[None, :]


def kernel(x, p, positions, g_in, w_in, q_norm, k_norm, w_spatial, b_spatial, ln_v_g, ln_v_b, w_out,
           g_ple, w_ple_gate, b_ple_gate, w_ple_proj):
    B, S, _ = x.shape
    depth = w_in.shape[0]
    bias = _band_bias()
    avg = _head_average()
    expand = _stat_expand()
    inv_freq = _inv_freq_row()
    perms = {r: _class_perms(r) for r in STRIDES}
    pos_f = positions.astype(F32)[..., None]
    h = x
    for i in range(depth):
        q, k, v, q4, k4, v4, q16, k16, v16, sg, oa = _proj_call(
            h, pos_f, inv_freq, g_in[i][None, :], w_in[i].astype(BF16), avg,
            jnp.tile(q_norm[i], N_HEADS)[None, :], jnp.tile(k_norm[i], N_HEADS)[None, :],
            w_spatial[i], jnp.repeat(b_spatial[i].T, LANES, axis=1),
            ln_v_g[i][None, :], ln_v_b[i][None, :], perms)
        o1, l1 = _attn_call(bias, q[:, None], k[:, None], v[:, None])
        o4, l4 = _attn_call(bias, q4, k4, v4)
        o16, l16 = _attn_call(bias, q16, k16, v16)
        h = _out_call(h, p[i], (o1[:, 0], o4, o16), (l1[:, 0], l4, l16), sg, oa, perms, expand,
                      w_out[i].astype(BF16), g_ple[i][None, :], w_ple_gate[i].astype(BF16),
                      b_ple_gate[i][None, :], w_ple_proj[i].astype(BF16))
    return h
```

```python
import math

import jax
import jax.numpy as jnp
import numpy as np
from jax import lax
from jax.experimental import pallas as pl
from jax.experimental.pallas import tpu as pltpu

D_MODEL = 1024
N_HEADS = 8
HEAD_DIM = 64
ATTN_WIDTH = N_HEADS * HEAD_DIM
DILATIONS = ((128, 1), (512, 4), (2048, 16))
BLK = 128
ROPE_THETA = 10000.0
GMLP_GROUPS = 4
GMLP_WIDTH = D_MODEL // 2
CHUNK = 128
PLE_DIM = 256
EPS = 1e-6
NEG = -1e30
LN2 = math.log(2.0)
LOG2E = math.log2(math.e)

LANES = 128
MXU_DEPTH = 256
N_PAIRS = ATTN_WIDTH // LANES
STAT_REP = LANES // N_HEADS
VMEM_LIMIT = 56 * 1024 * 1024

PROJ_ROWS = 512
ATTN_STEP_ROWS = 2048
ATTN_SUPER = 1024
OUT_ROWS = 1024
PERM_ROWS = MXU_DEPTH
STRIDES = tuple(d for _, d in DILATIONS if d > 1)

BF16 = jnp.bfloat16
F32 = jnp.float32


def _dot(a, b):
    return jnp.dot(a, b, preferred_element_type=F32)


def _dot_nt(a, b):
    return lax.dot_general(a, b, (((1,), (1,)), ((), ())), preferred_element_type=F32)


def _silu(x):
    return x * jax.nn.sigmoid(x)


def _project_pieces(x_ref, gin_ref, w_ref, z_ref):
    n_rows = x_ref.shape[1]
    half = n_rows // 2
    normed = {}

    def piece(h, j):
        rows = slice(h * half, (h + 1) * half)
        if h not in normed:
            x = x_ref[0, rows, :]
            scale = lax.rsqrt(jnp.mean(x * x, axis=-1, keepdims=True) + EPS)
            normed[h] = (x * scale * gin_ref[...]).astype(BF16)
        cols = slice(j * ATTN_WIDTH, (j + 1) * ATTN_WIDTH)
        z_ref[rows, cols] = _dot(normed[h], w_ref[:, cols])

    return [(lambda h=h, j=j: piece(h, j)) for h in range(2) for j in range(w_ref.shape[1] // ATTN_WIDTH)]


def _finish_pieces(z_ref, pos_ref, invf_ref, avg_ref, qg_ref, kg_ref, ws_ref, bs_ref, lng_ref, lnb_ref,
                   d4_ref, d16_ref, qkv_ref, qkv4_ref, qkv16_ref, sg_ref, oa_ref):
    n_rows = z_ref.shape[0]
    blocks = [slice(b * CHUNK, (b + 1) * CHUNK) for b in range(n_rows // CHUNK)]
    lane = lax.broadcasted_iota(jnp.int32, (1, LANES), 1)
    first_half = (lane % HEAD_DIM) < (HEAD_DIM // 2)
    cos, sin_signed, vn = {}, {}, {}

    def cols(j):
        return slice(j * ATTN_WIDTH, (j + 1) * ATTN_WIDTH)

    def tables(b):
        ang = pos_ref[0, blocks[b], :] * invf_ref[...]
        cos[b] = jnp.cos(ang)
        sin = jnp.sin(ang)
        sin_signed[b] = jnp.where(first_half, -sin, sin)

    def norm_rope(b, j, g_ref, scale):
        z = z_ref[blocks[b], cols(j)]
        sq = (z * z).astype(BF16)
        ms = jnp.concatenate([_dot(sq[:, c:c + MXU_DEPTH], avg_ref[c:c + MXU_DEPTH, c:c + MXU_DEPTH])
                              for c in range(0, ATTN_WIDTH, MXU_DEPTH)], axis=1)
        y = z * lax.rsqrt(ms + EPS) * g_ref[...]
        outs = []
        for p in range(N_PAIRS):
            yp = y[:, p * LANES:(p + 1) * LANES]
            swapped = jnp.where(first_half, pltpu.roll(yp, LANES - HEAD_DIM // 2, 1),
                                pltpu.roll(yp, HEAD_DIM // 2, 1))
            outs.append(yp * cos[b] + swapped * sin_signed[b])
        r = jnp.concatenate(outs, axis=1)
        if scale != 1.0:
            r = r * scale
        qkv_ref[0, blocks[b], cols(j)] = r.astype(BF16)

    def value(b):
        qkv_ref[0, blocks[b], cols(2)] = z_ref[blocks[b], cols(2)].astype(BF16)

    def regroup(j, part):
        piece = qkv_ref[0, j * PERM_ROWS:(j + 1) * PERM_ROWS, cols(part)]
        for (d_ref, out_ref) in ((d4_ref, qkv4_ref), (d16_ref, qkv16_ref)):
            r = out_ref.shape[1]
            n = PERM_ROWS // r
            cm = _dot(d_ref[...], piece).astype(BF16)
            for c in range(r):
                out_ref[0, c, j * n:(j + 1) * n, cols(part)] = cm[c * n:(c + 1) * n]

    def gate(b):
        sg_ref[0, blocks[b], :] = _silu(z_ref[blocks[b], cols(3)]).astype(BF16)

    def gmlp_pre(b):
        gv = jax.nn.gelu(z_ref[blocks[b], cols(5)])
        mu = jnp.mean(gv, axis=-1, keepdims=True)
        gc = gv - mu
        var = jnp.mean(gc * gc, axis=-1, keepdims=True)
        vn[b] = (gc * lax.rsqrt(var + EPS) * lng_ref[...] + lnb_ref[...]).astype(BF16)

    def gmlp_post(b):
        r_i = lax.broadcasted_iota(jnp.int32, (CHUNK, CHUNK), 0)
        c_i = lax.broadcasted_iota(jnp.int32, (CHUNK, CHUNK), 1)
        sv = jnp.concatenate(
            [_dot(jnp.where(c_i <= r_i, ws_ref[g], 0.0).astype(BF16), vn[b][:, g * LANES:(g + 1) * LANES])
             for g in range(GMLP_GROUPS)], axis=1) + bs_ref[...]
        u = jax.nn.gelu(z_ref[blocks[b], cols(4)])
        ga = _silu(z_ref[blocks[b], cols(6)])
        oa_ref[0, blocks[b], :] = (u * sv * ga).astype(BF16)

    nb = range(len(blocks))
    nc = range(n_rows // PERM_ROWS)
    pieces = []
    for b in nb:
        pieces += [(8.0, lambda b=b: tables(b)),
                   (3.0, lambda b=b: norm_rope(b, 0, qg_ref, HEAD_DIM ** -0.5 * LOG2E)),
                   (3.0, lambda b=b: norm_rope(b, 1, kg_ref, 1.0)),
                   (0.5, lambda b=b: value(b))]
    pieces += [(1.0, lambda b=b: gate(b)) for b in nb]
    pieces += [(2.0, lambda j=j: regroup(j, 0)) for j in nc]
    pieces += [(3.5, lambda b=b: gmlp_pre(b)) for b in nb]
    pieces += [(2.0, lambda j=j: regroup(j, 1)) for j in nc]
    pieces += [(2.0, lambda j=j: regroup(j, 2)) for j in nc]
    pieces += [(4.0, lambda b=b: gmlp_post(b)) for b in nb]
    return pieces


def _interleave(weighted, uniform):
    total = sum(w for w, _ in weighted)
    out, taken, done = [], 0, 0.0
    for w, piece in weighted:
        while taken < len(uniform) and done >= (taken + 0.5) * total / len(uniform):
            out.append(uniform[taken])
            taken += 1
        out.append(piece)
        done += w
    return out + uniform[taken:]


def _proj_kernel(x_ref, pos_ref, invf_ref, gin_ref, w_ref, *rest):
    finish_refs, (z0_ref, z1_ref) = rest[:-2], rest[-2:]
    step = pl.program_id(0)

    @pl.when(step == 0)
    def _():
        z1_ref[...] = jnp.zeros_like(z1_ref)

    def run(z_write_ref, z_read_ref):
        for piece in _interleave(_finish_pieces(z_read_ref, pos_ref, invf_ref, *finish_refs),
                                 _project_pieces(x_ref, gin_ref, w_ref, z_write_ref)):
            piece()

    @pl.when(step % 2 == 0)
    def _():
        run(z0_ref, z1_ref)

    @pl.when(step % 2 == 1)
    def _():
        run(z1_ref, z0_ref)


def _proj_call(x, pos_f, inv_freq, g_in, w_in, avg, qg, kg, w_s, b_tab, ln_g, ln_b, perms):
    B, S, D = x.shape
    T = PROJ_ROWS
    n_cols = w_in.shape[1]
    W = ATTN_WIDTH
    per_seq = S // T
    n_tiles = B * per_seq
    proj_tile = lambda s: jnp.minimum(s, n_tiles - 1)
    done_tile = lambda s: jnp.maximum(s - 1, 0)
    tok_in = pl.BlockSpec((1, T, D), lambda s: (proj_tile(s) // per_seq, proj_tile(s) % per_seq, 0))
    tok = lambda width: pl.BlockSpec((1, T, width), lambda s: (done_tile(s) // per_seq, done_tile(s) % per_seq, 0))
    cls = lambda r: pl.BlockSpec((1, r, T // r, 3 * W),
                                 lambda s: (done_tile(s) // per_seq, 0, done_tile(s) % per_seq, 0))
    const = lambda shape: pl.BlockSpec(shape, lambda s: (0,) * len(shape))
    nat_sds = lambda width: jax.ShapeDtypeStruct((B, S, width), BF16)
    cls_sds = lambda r: jax.ShapeDtypeStruct((B, r, S // r, 3 * W), BF16)
    r4, r16 = STRIDES
    return pl.pallas_call(
        _proj_kernel,
        grid=(n_tiles + 1,),
        scratch_shapes=[pltpu.VMEM((T, n_cols), F32), pltpu.VMEM((T, n_cols), F32)],
        in_specs=[tok_in, tok(1), const((1, LANES)), const((1, D)), const((D, n_cols)),
                  const((W, W)), const((1, W)), const((1, W)),
                  const((GMLP_GROUPS, CHUNK, CHUNK)), const((CHUNK, GMLP_WIDTH)),
                  const((1, GMLP_WIDTH)), const((1, GMLP_WIDTH)),
                  const((PERM_ROWS, PERM_ROWS)), const((PERM_ROWS, PERM_ROWS))],
        out_specs=[tok(3 * W), cls(r4), cls(r16), tok(W), tok(W)],
        out_shape=[nat_sds(3 * W), cls_sds(r4), cls_sds(r16), nat_sds(W), nat_sds(W)],
        compiler_params=pltpu.CompilerParams(dimension_semantics=("arbitrary",), vmem_limit_bytes=VMEM_LIMIT),
        name="proj",
    )(x, pos_f, inv_freq, g_in, w_in, avg, qg, kg, w_s, b_tab, ln_g, ln_b, perms[r4][0], perms[r16][0])


def _attn_kernel(bias_ref, q_ref, k_ref, v_ref, kp_ref, vp_ref, o_ref, l_ref, kk_ref, vv_ref):
    n_cls, n_rows = q_ref.shape[0], q_ref.shape[1]
    kk_ref[:, :BLK, :] = kp_ref[...]
    kk_ref[:, BLK:, :] = k_ref[...]
    vv_ref[:, :BLK, :] = vp_ref[...]
    vv_ref[:, BLK:, :] = v_ref[...]

    first_tile = pl.program_id(2) == 0
    lane = lax.broadcasted_iota(jnp.int32, (1, LANES), 1)
    in_a = lane < HEAD_DIM
    group = lane // STAT_REP % N_PAIRS
    ones = jnp.ones((2 * BLK, LANES), BF16)
    blocks_per_cls = n_rows // BLK
    blocks_per_it = ATTN_SUPER // BLK

    def super_block(it, carry):
        block = [it * blocks_per_it + s for s in range(blocks_per_it)]
        cls_of = [f // blocks_per_cls for f in block]
        first_of = [f % blocks_per_cls for f in block]
        units = [(s, p) for s in range(blocks_per_it) for p in range(N_PAIRS)]
        st = [dict() for _ in units]
        packed = {}

        def row_of(s):
            return pl.multiple_of(first_of[s] * BLK, BLK)

        def lanes_of(p):
            return slice(p * LANES, (p + 1) * LANES)

        def scores(u):
            s, p = units[u]
            qp = q_ref[cls_of[s], pl.ds(row_of(s), BLK), lanes_of(p)]
            kk = kk_ref[cls_of[s], pl.ds(row_of(s), 2 * BLK), lanes_of(p)]
            zero = jnp.zeros_like(qp)
            q2 = jnp.concatenate([jnp.where(in_a, qp, zero), jnp.where(in_a, zero, qp)], axis=0)
            st[u]["sc"] = _dot_nt(q2, kk)

        def exponentials(u):
            s, _ = units[u]
            if s % blocks_per_cls == 0:
                bias = bias_ref[jnp.logical_and(first_tile, first_of[s] == 0).astype(jnp.int32)]
            else:
                bias = bias_ref[0]
            sc = st[u].pop("sc") + bias
            m = jnp.max(sc, axis=-1, keepdims=True)
            st[u]["m"] = m
            st[u]["e"] = jnp.exp2(sc - m).astype(BF16)

        def weighted_values(u):
            s, p = units[u]
            vv = vv_ref[cls_of[s], pl.ds(row_of(s), 2 * BLK), lanes_of(p)]
            st[u]["res"] = _dot(st[u].pop("e"), jnp.concatenate([vv, ones], axis=1))

        def normalize(u):
            s, p = units[u]
            res, m = st[u].pop("res"), st[u].pop("m")
            num = jnp.where(in_a, res[:BLK, :LANES], res[BLK:, :LANES])
            den = jnp.where(in_a, res[:BLK, LANES:], res[BLK:, LANES:])
            o_ref[cls_of[s], pl.ds(row_of(s), BLK), lanes_of(p)] = (num / den).astype(o_ref.dtype)
            lse = jnp.where(in_a, m[:BLK], m[BLK:]) * LN2 + jnp.log(den)
            packed[s] = lse if p == 0 else jnp.where(group == p, lse, packed[s])
            if p == N_PAIRS - 1:
                l_ref[cls_of[s], pl.ds(row_of(s), BLK), :] = packed.pop(s)

        stages = (scores, exponentials, weighted_values, normalize)
        for t in range(len(units) + len(stages) - 1):
            for k, stage in enumerate(stages):
                if 0 <= t - k < len(units):
                    stage(t - k)
        return carry

    lax.fori_loop(0, n_cls * blocks_per_cls // blocks_per_it, super_block, 0)


def _attn_call(bias, qkv):
    B, r, L, W = qkv.shape[0], qkv.shape[1], qkv.shape[2], qkv.shape[3] // 3
    R = min(ATTN_STEP_ROWS, L)
    CB = min(r, ATTN_STEP_ROWS // R)
    cur = lambda width, part=0: pl.BlockSpec((None, CB, R, width), lambda b, g, i: (b, g, i, part))
    prev = lambda part: pl.BlockSpec((None, CB, BLK, W),
                                     lambda b, g, i: (b, g, jnp.maximum(i * (R // BLK) - 1, 0), part))
    return pl.pallas_call(
        _attn_kernel,
        grid=(B, r // CB, L // R),
        in_specs=[pl.BlockSpec(bias.shape, lambda b, g, i: (0, 0, 0)), cur(W, 0), cur(W, 1), cur(W, 2),
                  prev(1), prev(2)],
        out_specs=[cur(W), cur(LANES)],
        out_shape=[jax.ShapeDtypeStruct((B, r, L, W), BF16), jax.ShapeDtypeStruct((B, r, L, LANES), F32)],
        scratch_shapes=[pltpu.VMEM((CB, R + BLK, W), BF16), pltpu.VMEM((CB, R + BLK, W), BF16)],
        compiler_params=pltpu.CompilerParams(
            dimension_semantics=("arbitrary", "arbitrary", "arbitrary"), vmem_limit_bytes=VMEM_LIMIT),
        name=f"attn_d{r}",
    )(bias, qkv, qkv, qkv, qkv, qkv)


def _out_kernel(x_ref, p_ref, o1_ref, o4_ref, o16_ref, l1_ref, l4_ref, l16_ref, sg_ref, oa_ref,
                u4_ref, u16_ref, exp_ref, wo_ref, gple_ref, wg_ref, bg_ref, wp_ref, out_ref):
    n_blocks = x_ref.shape[1] // PERM_ROWS
    st = [dict() for _ in range(n_blocks)]

    def rows(b):
        return slice(b * PERM_ROWS, (b + 1) * PERM_ROWS)

    def grouped(ref, b):
        r = ref.shape[0]
        n = PERM_ROWS // r
        return jnp.concatenate([ref[c, b * n:(b + 1) * n, :] for c in range(r)], axis=0)

    def to_token_order(b):
        s = st[b]
        s["o4"] = _dot(u4_ref[...], grouped(o4_ref, b))
        s["o16"] = _dot(u16_ref[...], grouped(o16_ref, b))
        for name, ref, u_ref in (("l4", l4_ref, u4_ref), ("l16", l16_ref, u16_ref)):
            cm = grouped(ref, b)
            hi = cm.astype(BF16)
            lo = (cm - hi.astype(F32)).astype(BF16)
            t = _dot(u_ref[...], jnp.concatenate([hi, lo], axis=1))
            s[name] = t[:, :LANES] + t[:, LANES:]

    def merge_weights(b):
        s = st[b]
        l1 = l1_ref[0, rows(b), :]
        m = jnp.maximum(jnp.maximum(l1, s["l4"]), s["l16"])
        e1, e4, e16 = jnp.exp(l1 - m), jnp.exp(s["l4"] - m), jnp.exp(s["l16"] - m)
        inv = 1.0 / (e1 + e4 + e16)
        s["w"] = []
        for e in (e1, e4):
            w = e * inv
            w_hi = w.astype(BF16)
            w_lo = (w - w_hi.astype(F32)).astype(BF16)
            s["w"].append(jnp.concatenate([w_hi, w_lo], axis=1))

    def expand_weights(b):
        s = st[b]
        s["wf"] = [_dot(w, exp_ref[...]) for w in s["w"]]

    def merge(b):
        s = st[b]
        o1 = o1_ref[0, rows(b), :].astype(F32)
        ob = s["o16"] + s["wf"][0] * (o1 - s["o16"]) + s["wf"][1] * (s["o4"] - s["o16"])
        s["mb"] = (ob * sg_ref[0, rows(b), :].astype(F32)).astype(BF16)

    def project(b):
        s = st[b]
        s["t"] = _dot(s["mb"], wo_ref[:ATTN_WIDTH, :]) + _dot(oa_ref[0, rows(b), :], wo_ref[ATTN_WIDTH:, :])

    def residual_norm(b):
        s = st[b]
        h = x_ref[0, rows(b), :] + s["t"]
        s["h"] = h
        s["n"] = (h * lax.rsqrt(jnp.mean(h * h, axis=-1, keepdims=True) + EPS) * gple_ref[...]).astype(BF16)

    def embed_dots(b):
        s = st[b]
        s["g"] = _dot(s["n"], wg_ref[...])
        s["pp"] = _dot(p_ref[0, rows(b), :].astype(BF16), wp_ref[...])

    def embed_out(b):
        s = st[b]
        out_ref[0, rows(b), :] = s["h"] + jax.nn.sigmoid(s["g"] + bg_ref[...]) * s["pp"]

    stages = (to_token_order, merge_weights, expand_weights, merge, project, residual_norm, embed_dots, embed_out)
    for t in range(len(stages) + n_blocks - 1):
        for b in range(n_blocks):
            if 0 <= t - b < len(stages):
                stages[t - b](b)


def _out_call(x, p, os_, ls_, sg, oa, perms, expand, w_out, g_ple, w_gate, b_gate, w_proj):
    B, S, D = x.shape
    T = OUT_ROWS
    tok = lambda width: pl.BlockSpec((1, T, width), lambda b, i: (b, i, 0))
    cls = lambda r, width: pl.BlockSpec((None, r, T // r, width), lambda b, i: (b, 0, i, 0))
    const = lambda shape: pl.BlockSpec(shape, lambda b, i: (0,) * len(shape))
    r4, r16 = STRIDES
    W = ATTN_WIDTH
    return pl.pallas_call(
        _out_kernel,
        grid=(B, S // T),
        in_specs=[tok(D), tok(PLE_DIM), tok(W), cls(r4, W), cls(r16, W), tok(LANES), cls(r4, LANES),
                  cls(r16, LANES), tok(W), tok(GMLP_WIDTH),
                  const((PERM_ROWS, PERM_ROWS)), const((PERM_ROWS, PERM_ROWS)), const(expand.shape),
                  const(w_out.shape), const((1, D)), const(w_gate.shape), const((1, D)), const(w_proj.shape)],
        out_specs=tok(D),
        out_shape=jax.ShapeDtypeStruct((B, S, D), x.dtype),
        compiler_params=pltpu.CompilerParams(
            dimension_semantics=("arbitrary", "arbitrary"), vmem_limit_bytes=VMEM_LIMIT),
        name="out",
    )(x, p, *os_, *ls_, sg, oa, perms[r4][1], perms[r16][1], expand, w_out, g_ple, w_gate, b_gate, w_proj)


def _band_bias():
    qi = np.arange(BLK)[:, None]
    col = np.arange(2 * BLK)[None, :]
    dist = qi + BLK - col
    band = (dist >= 0) & (dist <= BLK)
    both = np.stack([band, band & (col >= BLK)])
    bias = np.where(both, 0.0, NEG).astype(np.float32)
    return jnp.asarray(np.concatenate([bias, bias], axis=1))


def _head_average():
    h = np.arange(ATTN_WIDTH) // HEAD_DIM
    return jnp.asarray((h[:, None] == h[None, :]).astype(np.float32) / HEAD_DIM, dtype=BF16)


def _stat_expand():
    lane = np.arange(LANES)
    g = lane // STAT_REP
    head = np.where(g < N_PAIRS, 2 * g, 2 * (g - N_PAIRS) + 1)
    pick = (lane % STAT_REP) == 0
    col_head = np.arange(ATTN_WIDTH) // HEAD_DIM
    e = ((head[:, None] == col_head[None, :]) & pick[:, None]).astype(np.float32)
    return jnp.asarray(np.concatenate([e, e], axis=0), dtype=BF16)


def _class_perms(r):
    t = np.arange(PERM_ROWS)
    grouped_pos = (t % r) * (PERM_ROWS // r) + t // r
    d = np.zeros((PERM_ROWS, PERM_ROWS), np.float32)
    d[grouped_pos, t] = 1.0
    return jnp.asarray(d, dtype=BF16), jnp.asarray(d.T, dtype=BF16)


def _inv_freq_row():
    half = HEAD_DIM // 2
    inv = jnp.exp(-math.log(ROPE_THETA) * jnp.arange(half, dtype=F32) / half)
    return jnp.tile(inv, LANES // half)[None, :]


def kernel(x, p, positions, g_in, w_in, q_norm, k_norm, w_spatial, b_spatial, ln_v_g, ln_v_b, w_out,
           g_ple, w_ple_gate, b_ple_gate, w_ple_proj):
    B, S, _ = x.shape
    depth = w_in.shape[0]
    bias = _band_bias()
    avg = _head_average()
    expand = _stat_expand()
    inv_freq = _inv_freq_row()
    perms = {r: _class_perms(r) for r in STRIDES}
    pos_f = positions.astype(F32)[..., None]
    h = x
    for i in range(depth):
        qkv, qkv4, qkv16, sg, oa = _proj_call(
            h, pos_f, inv_freq, g_in[i][None, :], w_in[i].astype(BF16), avg,
            jnp.tile(q_norm[i], N_HEADS)[None, :], jnp.tile(k_norm[i], N_HEADS)[None, :],
            w_spatial[i], jnp.repeat(b_spatial[i].T, LANES, axis=1),
            ln_v_g[i][None, :], ln_v_b[i][None, :], perms)
        o1, l1 = _attn_call(bias, qkv[:, None])
        o4, l4 = _attn_call(bias, qkv4)
        o16, l16 = _attn_call(bias, qkv16)
        h = _out_call(h, p[i], (o1[:, 0], o4, o16), (l1[:, 0], l4, l16), sg, oa, perms, expand,
                      w_out[i].astype(BF16), g_ple[i][None, :], w_ple_gate[i].astype(BF16),
                      b_ple_gate[i][None, :], w_ple_proj[i].astype(BF16))
    return h
```

```python
import math

import jax
import jax.numpy as jnp
import numpy as np
from jax import lax
from jax.experimental import pallas as pl
from jax.experimental.pallas import tpu as pltpu

D_MODEL = 1024
N_HEADS = 8
HEAD_DIM = 64
ATTN_WIDTH = N_HEADS * HEAD_DIM
DILATIONS = ((128, 1), (512, 4), (2048, 16))
BLK = 128
ROPE_THETA = 10000.0
GMLP_GROUPS = 4
GMLP_WIDTH = D_MODEL // 2
CHUNK = 128
PLE_DIM = 256
EPS = 1e-6
NEG = -1e30
LN2 = math.log(2.0)
LOG2E = math.log2(math.e)

LANES = 128
MXU_DEPTH = 256
N_PAIRS = ATTN_WIDTH // LANES
STAT_REP = LANES // N_HEADS
VMEM_LIMIT = 56 * 1024 * 1024

PROJ_ROWS = 512
ATTN_STEP_ROWS = 2048
ATTN_SUPER = 1024
OUT_ROWS = 1024
PERM_ROWS = 256
STRIDES = tuple(d for _, d in DILATIONS if d > 1)
SUBSTRIDE = STRIDES[0]
assert STRIDES == (SUBSTRIDE, SUBSTRIDE ** 2)

BF16 = jnp.bfloat16
F32 = jnp.float32


def _dot(a, b):
    return jnp.dot(a, b, preferred_element_type=F32)


def _dot_nt(a, b):
    return lax.dot_general(a, b, (((1,), (1,)), ((), ())), preferred_element_type=F32)


def _silu(x):
    return x * jax.nn.sigmoid(x)


def _project_pieces(x_ref, gin_ref, w_ref, z_ref):
    n_rows = x_ref.shape[1]
    half = n_rows // 2
    normed = {}

    def piece(h, j):
        rows = slice(h * half, (h + 1) * half)
        if h not in normed:
            x = x_ref[0, rows, :]
            scale = lax.rsqrt(jnp.mean(x * x, axis=-1, keepdims=True) + EPS)
            normed[h] = (x * scale * gin_ref[...]).astype(BF16)
        cols = slice(j * ATTN_WIDTH, (j + 1) * ATTN_WIDTH)
        z_ref[rows, cols] = _dot(normed[h], w_ref[:, cols])

    return [(lambda h=h, j=j: piece(h, j)) for h in range(2) for j in range(w_ref.shape[1] // ATTN_WIDTH)]


def _finish_pieces(z_ref, pos_ref, invf_ref, avg_ref, qg_ref, kg_ref, ws_ref, bs_ref, lng_ref, lnb_ref,
                   qkv_ref, qkv4_ref, qkv16_ref, sg_ref, oa_ref, s1_ref, s2_ref):
    n_rows = z_ref.shape[0]
    blocks = [slice(b * CHUNK, (b + 1) * CHUNK) for b in range(n_rows // CHUNK)]
    lane = lax.broadcasted_iota(jnp.int32, (1, LANES), 1)
    first_half = (lane % HEAD_DIM) < (HEAD_DIM // 2)
    cos, sin_signed, vn = {}, {}, {}

    def cols(j):
        return slice(j * ATTN_WIDTH, (j + 1) * ATTN_WIDTH)

    def tables(b):
        ang = pos_ref[0, blocks[b], :] * invf_ref[...]
        cos[b] = jnp.cos(ang)
        sin = jnp.sin(ang)
        sin_signed[b] = jnp.where(first_half, -sin, sin)

    def norm_rope(b, j, g_ref, scale):
        z = z_ref[blocks[b], cols(j)]
        sq = (z * z).astype(BF16)
        ms = jnp.concatenate([_dot(sq[:, c:c + MXU_DEPTH], avg_ref[c:c + MXU_DEPTH, c:c + MXU_DEPTH])
                              for c in range(0, ATTN_WIDTH, MXU_DEPTH)], axis=1)
        y = z * lax.rsqrt(ms + EPS) * g_ref[...]
        outs = []
        for p in range(N_PAIRS):
            yp = y[:, p * LANES:(p + 1) * LANES]
            swapped = jnp.where(first_half, pltpu.roll(yp, LANES - HEAD_DIM // 2, 1),
                                pltpu.roll(yp, HEAD_DIM // 2, 1))
            outs.append(yp * cos[b] + swapped * sin_signed[b])
        r = jnp.concatenate(outs, axis=1)
        if scale != 1.0:
            r = r * scale
        emit(j, b, r)

    def emit(part, b, val):
        qkv_ref[0, blocks[b], cols(part)] = val.astype(BF16)
        chunk, off = divmod(b * CHUNK, PERM_ROWS)
        for p in range(N_PAIRS):
            s1_ref[part, chunk, p, off:off + CHUNK, :] = val[:, p * LANES:(p + 1) * LANES]

    def value(b):
        emit(2, b, z_ref[blocks[b], cols(2)])

    def regroup(j, part):
        n4 = PERM_ROWS // SUBSTRIDE
        n16 = n4 // SUBSTRIDE
        for p in range(N_PAIRS):
            lanes = slice(part * ATTN_WIDTH + p * LANES, part * ATTN_WIDTH + (p + 1) * LANES)
            for c0 in range(SUBSTRIDE):
                g = s1_ref[part, j, p, pl.ds(c0, n4, stride=SUBSTRIDE), :]
                qkv4_ref[0, c0, j * n4:(j + 1) * n4, lanes] = g.astype(BF16)
                s2_ref[part, j, p, c0 * n4:(c0 + 1) * n4, :] = g
            for c0 in range(SUBSTRIDE):
                for c1 in range(SUBSTRIDE):
                    h = s2_ref[part, j, p, pl.ds(c0 * n4 + c1, n16, stride=SUBSTRIDE), :]
                    qkv16_ref[0, c0 + SUBSTRIDE * c1, j * n16:(j + 1) * n16, lanes] = h.astype(BF16)

    def gate(b):
        sg_ref[0, blocks[b], :] = _silu(z_ref[blocks[b], cols(3)]).astype(BF16)

    def gmlp_pre(b):
        gv = jax.nn.gelu(z_ref[blocks[b], cols(5)])
        mu = jnp.mean(gv, axis=-1, keepdims=True)
        gc = gv - mu
        var = jnp.mean(gc * gc, axis=-1, keepdims=True)
        vn[b] = (gc * lax.rsqrt(var + EPS) * lng_ref[...] + lnb_ref[...]).astype(BF16)

    def gmlp_post(b):
        r_i = lax.broadcasted_iota(jnp.int32, (CHUNK, CHUNK), 0)
        c_i = lax.broadcasted_iota(jnp.int32, (CHUNK, CHUNK), 1)
        sv = jnp.concatenate(
            [_dot(jnp.where(c_i <= r_i, ws_ref[g], 0.0).astype(BF16), vn[b][:, g * LANES:(g + 1) * LANES])
             for g in range(GMLP_GROUPS)], axis=1) + bs_ref[...]
        u = jax.nn.gelu(z_ref[blocks[b], cols(4)])
        ga = _silu(z_ref[blocks[b], cols(6)])
        oa_ref[0, blocks[b], :] = (u * sv * ga).astype(BF16)

    nb = range(len(blocks))
    nc = range(n_rows // PERM_ROWS)
    pieces = []
    for b in nb:
        pieces += [(8.0, lambda b=b: tables(b)),
                   (3.0, lambda b=b: norm_rope(b, 0, qg_ref, HEAD_DIM ** -0.5 * LOG2E)),
                   (3.0, lambda b=b: norm_rope(b, 1, kg_ref, 1.0)),
                   (0.5, lambda b=b: value(b))]
    pieces += [(1.0, lambda b=b: gate(b)) for b in nb]
    pieces += [(3.0, lambda j=j: regroup(j, 0)) for j in nc]
    pieces += [(3.5, lambda b=b: gmlp_pre(b)) for b in nb]
    pieces += [(3.0, lambda j=j: regroup(j, 1)) for j in nc]
    pieces += [(3.0, lambda j=j: regroup(j, 2)) for j in nc]
    pieces += [(4.0, lambda b=b: gmlp_post(b)) for b in nb]
    return pieces


def _interleave(weighted, uniform):
    total = sum(w for w, _ in weighted)
    out, taken, done = [], 0, 0.0
    for w, piece in weighted:
        while taken < len(uniform) and done >= (taken + 0.5) * total / len(uniform):
            out.append(uniform[taken])
            taken += 1
        out.append(piece)
        done += w
    return out + uniform[taken:]


def _proj_kernel(x_ref, pos_ref, invf_ref, gin_ref, w_ref, *rest):
    finish_refs, (z0_ref, z1_ref) = rest[:-4] + rest[-2:], rest[-4:-2]
    step = pl.program_id(0)

    @pl.when(step == 0)
    def _():
        z1_ref[...] = jnp.zeros_like(z1_ref)

    def run(z_write_ref, z_read_ref):
        for piece in _interleave(_finish_pieces(z_read_ref, pos_ref, invf_ref, *finish_refs),
                                 _project_pieces(x_ref, gin_ref, w_ref, z_write_ref)):
            piece()

    @pl.when(step % 2 == 0)
    def _():
        run(z0_ref, z1_ref)

    @pl.when(step % 2 == 1)
    def _():
        run(z1_ref, z0_ref)


def _proj_call(x, pos_f, inv_freq, g_in, w_in, avg, qg, kg, w_s, b_tab, ln_g, ln_b):
    B, S, D = x.shape
    T = PROJ_ROWS
    n_cols = w_in.shape[1]
    W = ATTN_WIDTH
    per_seq = S // T
    n_tiles = B * per_seq
    proj_tile = lambda s: jnp.minimum(s, n_tiles - 1)
    done_tile = lambda s: jnp.maximum(s - 1, 0)
    tok_in = pl.BlockSpec((1, T, D), lambda s: (proj_tile(s) // per_seq, proj_tile(s) % per_seq, 0))
    tok = lambda width: pl.BlockSpec((1, T, width), lambda s: (done_tile(s) // per_seq, done_tile(s) % per_seq, 0))
    cls = lambda r: pl.BlockSpec((1, r, T // r, 3 * W),
                                 lambda s: (done_tile(s) // per_seq, 0, done_tile(s) % per_seq, 0))
    const = lambda shape: pl.BlockSpec(shape, lambda s: (0,) * len(shape))
    nat_sds = lambda width: jax.ShapeDtypeStruct((B, S, width), BF16)
    cls_sds = lambda r: jax.ShapeDtypeStruct((B, r, S // r, 3 * W), BF16)
    r4, r16 = STRIDES
    return pl.pallas_call(
        _proj_kernel,
        grid=(n_tiles + 1,),
        scratch_shapes=[pltpu.VMEM((T, n_cols), F32), pltpu.VMEM((T, n_cols), F32),
                        pltpu.VMEM((3, T // PERM_ROWS, N_PAIRS, PERM_ROWS, LANES), F32),
                        pltpu.VMEM((3, T // PERM_ROWS, N_PAIRS, PERM_ROWS, LANES), F32)],
        in_specs=[tok_in, tok(1), const((1, LANES)), const((1, D)), const((D, n_cols)),
                  const((W, W)), const((1, W)), const((1, W)),
                  const((GMLP_GROUPS, CHUNK, CHUNK)), const((CHUNK, GMLP_WIDTH)),
                  const((1, GMLP_WIDTH)), const((1, GMLP_WIDTH))],
        out_specs=[tok(3 * W), cls(r4), cls(r16), tok(W), tok(W)],
        out_shape=[nat_sds(3 * W), cls_sds(r4), cls_sds(r16), nat_sds(W), nat_sds(W)],
        compiler_params=pltpu.CompilerParams(dimension_semantics=("arbitrary",), vmem_limit_bytes=VMEM_LIMIT),
        name="proj",
    )(x, pos_f, inv_freq, g_in, w_in, avg, qg, kg, w_s, b_tab, ln_g, ln_b)


def _attn_kernel(bias_ref, q_ref, k_ref, v_ref, kp_ref, vp_ref, o_ref, l_ref, kk_ref, vv_ref):
    n_cls, n_rows = q_ref.shape[0], q_ref.shape[1]
    kk_ref[:, :BLK, :] = kp_ref[...]
    kk_ref[:, BLK:, :] = k_ref[...]
    vv_ref[:, :BLK, :] = vp_ref[...]
    vv_ref[:, BLK:, :] = v_ref[...]

    first_tile = pl.program_id(2) == 0
    lane = lax.broadcasted_iota(jnp.int32, (1, LANES), 1)
    in_a = lane < HEAD_DIM
    group = lane // STAT_REP % N_PAIRS
    ones = jnp.ones((2 * BLK, LANES), BF16)
    blocks_per_cls = n_rows // BLK
    blocks_per_it = ATTN_SUPER // BLK

    def super_block(it, carry):
        block = [it * blocks_per_it + s for s in range(blocks_per_it)]
        cls_of = [f // blocks_per_cls for f in block]
        first_of = [f % blocks_per_cls for f in block]
        units = [(s, p) for s in range(blocks_per_it) for p in range(N_PAIRS)]
        st = [dict() for _ in units]
        packed = {}

        def row_of(s):
            return pl.multiple_of(first_of[s] * BLK, BLK)

        def lanes_of(p):
            return slice(p * LANES, (p + 1) * LANES)

        def scores(u):
            s, p = units[u]
            qp = q_ref[cls_of[s], pl.ds(row_of(s), BLK), lanes_of(p)]
            kk = kk_ref[cls_of[s], pl.ds(row_of(s), 2 * BLK), lanes_of(p)]
            zero = jnp.zeros_like(qp)
            q2 = jnp.concatenate([jnp.where(in_a, qp, zero), jnp.where(in_a, zero, qp)], axis=0)
            st[u]["sc"] = _dot_nt(q2, kk)

        def exponentials(u):
            s, _ = units[u]
            if s % blocks_per_cls == 0:
                bias = bias_ref[jnp.logical_and(first_tile, first_of[s] == 0).astype(jnp.int32)]
            else:
                bias = bias_ref[0]
            sc = st[u].pop("sc") + bias
            m = jnp.max(sc, axis=-1, keepdims=True)
            st[u]["m"] = m
            st[u]["e"] = jnp.exp2(sc - m).astype(BF16)

        def weighted_values(u):
            s, p = units[u]
            vv = vv_ref[cls_of[s], pl.ds(row_of(s), 2 * BLK), lanes_of(p)]
            st[u]["res"] = _dot(st[u].pop("e"), jnp.concatenate([vv, ones], axis=1))

        def normalize(u):
            s, p = units[u]
            res, m = st[u].pop("res"), st[u].pop("m")
            num = jnp.where(in_a, res[:BLK, :LANES], res[BLK:, :LANES])
            den = jnp.where(in_a, res[:BLK, LANES:], res[BLK:, LANES:])
            o_ref[cls_of[s], pl.ds(row_of(s), BLK), lanes_of(p)] = (num / den).astype(o_ref.dtype)
            lse = jnp.where(in_a, m[:BLK], m[BLK:]) * LN2 + jnp.log(den)
            packed[s] = lse if p == 0 else jnp.where(group == p, lse, packed[s])
            if p == N_PAIRS - 1:
                l_ref[cls_of[s], pl.ds(row_of(s), BLK), :] = packed.pop(s)

        stages = (scores, exponentials, weighted_values, normalize)
        for t in range(len(units) + len(stages) - 1):
            for k, stage in enumerate(stages):
                if 0 <= t - k < len(units):
                    stage(t - k)
        return carry

    lax.fori_loop(0, n_cls * blocks_per_cls // blocks_per_it, super_block, 0)


def _attn_call(bias, qkv):
    B, r, L, W = qkv.shape[0], qkv.shape[1], qkv.shape[2], qkv.shape[3] // 3
    R = min(ATTN_STEP_ROWS, L)
    CB = min(r, ATTN_STEP_ROWS // R)
    cur = lambda width, part=0: pl.BlockSpec((None, CB, R, width), lambda b, g, i: (b, g, i, part))
    prev = lambda part: pl.BlockSpec((None, CB, BLK, W),
                                     lambda b, g, i: (b, g, jnp.maximum(i * (R // BLK) - 1, 0), part))
    return pl.pallas_call(
        _attn_kernel,
        grid=(B, r // CB, L // R),
        in_specs=[pl.BlockSpec(bias.shape, lambda b, g, i: (0, 0, 0)), cur(W, 0), cur(W, 1), cur(W, 2),
                  prev(1), prev(2)],
        out_specs=[cur(W), cur(LANES)],
        out_shape=[jax.ShapeDtypeStruct((B, r, L, W), BF16), jax.ShapeDtypeStruct((B, r, L, LANES), F32)],
        scratch_shapes=[pltpu.VMEM((CB, R + BLK, W), BF16), pltpu.VMEM((CB, R + BLK, W), BF16)],
        compiler_params=pltpu.CompilerParams(
            dimension_semantics=("arbitrary", "arbitrary", "arbitrary"), vmem_limit_bytes=VMEM_LIMIT),
        name=f"attn_d{r}",
    )(bias, qkv, qkv, qkv, qkv, qkv)


def _out_kernel(x_ref, p_ref, o1_ref, o4_ref, o16_ref, l1_ref, l4_ref, l16_ref, sg_ref, oa_ref,
                exp_ref, wo_ref, gple_ref, wg_ref, bg_ref, wp_ref, out_ref, tok_ref, mid_ref, ltok_ref, lmid_ref):
    n_blocks = x_ref.shape[1] // PERM_ROWS
    st = [dict() for _ in range(n_blocks)]

    def rows(b):
        return slice(b * PERM_ROWS, (b + 1) * PERM_ROWS)

    def to_token_order(b):
        s = st[b]
        n4 = PERM_ROWS // SUBSTRIDE
        n16 = n4 // SUBSTRIDE
        pair = [slice(p * LANES, (p + 1) * LANES) for p in range(N_PAIRS)]
        for c0 in range(SUBSTRIDE):
            src = slice(b * n4, (b + 1) * n4)
            dst = pl.ds(c0, n4, stride=SUBSTRIDE)
            ltok_ref[b, 0, dst, :] = l4_ref[c0, src, :]
            for p in range(N_PAIRS):
                tok_ref[b, 0, p, dst, :] = o4_ref[c0, src, pair[p]].astype(F32)
            for c1 in range(SUBSTRIDE):
                src = slice(b * n16, (b + 1) * n16)
                dst = pl.ds(c0 * n4 + c1, n16, stride=SUBSTRIDE)
                lmid_ref[b, dst, :] = l16_ref[c0 + SUBSTRIDE * c1, src, :]
                for p in range(N_PAIRS):
                    mid_ref[b, p, dst, :] = o16_ref[c0 + SUBSTRIDE * c1, src, pair[p]].astype(F32)
        for c0 in range(SUBSTRIDE):
            src = slice(c0 * n4, (c0 + 1) * n4)
            dst = pl.ds(c0, n4, stride=SUBSTRIDE)
            ltok_ref[b, 1, dst, :] = lmid_ref[b, src, :]
            for p in range(N_PAIRS):
                tok_ref[b, 1, p, dst, :] = mid_ref[b, p, src, :]
        s["l4"], s["l16"] = ltok_ref[b, 0], ltok_ref[b, 1]
        s["o4"] = jnp.concatenate([tok_ref[b, 0, p] for p in range(N_PAIRS)], axis=1)
        s["o16"] = jnp.concatenate([tok_ref[b, 1, p] for p in range(N_PAIRS)], axis=1)

    def merge_weights(b):
        s = st[b]
        l1 = l1_ref[0, rows(b), :]
        m = jnp.maximum(jnp.maximum(l1, s["l4"]), s["l16"])
        e1, e4, e16 = jnp.exp(l1 - m), jnp.exp(s["l4"] - m), jnp.exp(s["l16"] - m)
        inv = 1.0 / (e1 + e4 + e16)
        s["w"] = []
        for e in (e1, e4):
            w = e * inv
            w_hi = w.astype(BF16)
            w_lo = (w - w_hi.astype(F32)).astype(BF16)
            s["w"].append(jnp.concatenate([w_hi, w_lo], axis=1))

    def expand_weights(b):
        s = st[b]
        s["wf"] = [_dot(w, exp_ref[...]) for w in s["w"]]

    def merge(b):
        s = st[b]
        o1 = o1_ref[0, rows(b), :].astype(F32)
        ob = s["o16"] + s["wf"][0] * (o1 - s["o16"]) + s["wf"][1] * (s["o4"] - s["o16"])
        s["mb"] = (ob * sg_ref[0, rows(b), :].astype(F32)).astype(BF16)

    def project(b):
        s = st[b]
        s["t"] = _dot(s["mb"], wo_ref[:ATTN_WIDTH, :]) + _dot(oa_ref[0, rows(b), :], wo_ref[ATTN_WIDTH:, :])

    def residual_norm(b):
        s = st[b]
        h = x_ref[0, rows(b), :] + s["t"]
        s["h"] = h
        s["n"] = (h * lax.rsqrt(jnp.mean(h * h, axis=-1, keepdims=True) + EPS) * gple_ref[...]).astype(BF16)

    def embed_dots(b):
        s = st[b]
        s["g"] = _dot(s["n"], wg_ref[...])
        s["pp"] = _dot(p_ref[0, rows(b), :].astype(BF16), wp_ref[...])

    def embed_out(b):
        s = st[b]
        out_ref[0, rows(b), :] = s["h"] + jax.nn.sigmoid(s["g"] + bg_ref[...]) * s["pp"]

    stages = (to_token_order, merge_weights, expand_weights, merge, project, residual_norm, embed_dots, embed_out)
    for t in range(len(stages) + n_blocks - 1):
        for b in range(n_blocks):
            if 0 <= t - b < len(stages):
                stages[t - b](b)


def _out_call(x, p, os_, ls_, sg, oa, expand, w_out, g_ple, w_gate, b_gate, w_proj):
    B, S, D = x.shape
    T = OUT_ROWS
    tok = lambda width: pl.BlockSpec((1, T, width), lambda b, i: (b, i, 0))
    cls = lambda r, width: pl.BlockSpec((None, r, T // r, width), lambda b, i: (b, 0, i, 0))
    const = lambda shape: pl.BlockSpec(shape, lambda b, i: (0,) * len(shape))
    r4, r16 = STRIDES
    W = ATTN_WIDTH
    n_blocks = T // PERM_ROWS
    return pl.pallas_call(
        _out_kernel,
        grid=(B, S // T),
        in_specs=[tok(D), tok(PLE_DIM), tok(W), cls(r4, W), cls(r16, W), tok(LANES), cls(r4, LANES),
                  cls(r16, LANES), tok(W), tok(GMLP_WIDTH), const(expand.shape),
                  const(w_out.shape), const((1, D)), const(w_gate.shape), const((1, D)), const(w_proj.shape)],
        out_specs=tok(D),
        out_shape=jax.ShapeDtypeStruct((B, S, D), x.dtype),
        scratch_shapes=[pltpu.VMEM((n_blocks, 2, N_PAIRS, PERM_ROWS, LANES), F32),
                        pltpu.VMEM((n_blocks, N_PAIRS, PERM_ROWS, LANES), F32),
                        pltpu.VMEM((n_blocks, 2, PERM_ROWS, LANES), F32),
                        pltpu.VMEM((n_blocks, PERM_ROWS, LANES), F32)],
        compiler_params=pltpu.CompilerParams(
            dimension_semantics=("arbitrary", "arbitrary"), vmem_limit_bytes=VMEM_LIMIT),
        name="out",
    )(x, p, *os_, *ls_, sg, oa, expand, w_out, g_ple, w_gate, b_gate, w_proj)


def _band_bias():
    qi = np.arange(BLK)[:, None]
    col = np.arange(2 * BLK)[None, :]
    dist = qi + BLK - col
    band = (dist >= 0) & (dist <= BLK)
    both = np.stack([band, band & (col >= BLK)])
    bias = np.where(both, 0.0, NEG).astype(np.float32)
    return jnp.asarray(np.concatenate([bias, bias], axis=1))


def _head_average():
    h = np.arange(ATTN_WIDTH) // HEAD_DIM
    return jnp.asarray((h[:, None] == h[None, :]).astype(np.float32) / HEAD_DIM, dtype=BF16)


def _stat_expand():
    lane = np.arange(LANES)
    g = lane // STAT_REP
    head = np.where(g < N_PAIRS, 2 * g, 2 * (g - N_PAIRS) + 1)
    pick = (lane % STAT_REP) == 0
    col_head = np.arange(ATTN_WIDTH) // HEAD_DIM
    e = ((head[:, None] == col_head[None, :]) & pick[:, None]).astype(np.float32)
    return jnp.asarray(np.concatenate([e, e], axis=0), dtype=BF16)


def _inv_freq_row():
    half = HEAD_DIM // 2
    inv = jnp.exp(-math.log(ROPE_THETA) * jnp.arange(half, dtype=F32) / half)
    return jnp.tile(inv, LANES // half)[None, :]


def kernel(x, p, positions, g_in, w_in, q_norm, k_norm, w_spatial, b_spatial, ln_v_g, ln_v_b, w_out,
           g_ple, w_ple_gate, b_ple_gate, w_ple_proj):
    B, S, _ = x.shape
    depth = w_in.shape[0]
    bias = _band_bias()
    avg = _head_average()
    expand = _stat_expand()
    inv_freq = _inv_freq_row()
    pos_f = positions.astype(F32)[..., None]
    h = x
    for i in range(depth):
        qkv, qkv4, qkv16, sg, oa = _proj_call(
            h, pos_f, inv_freq, g_in[i][None, :], w_in[i].astype(BF16), avg,
            jnp.tile(q_norm[i], N_HEADS)[None, :], jnp.tile(k_norm[i], N_HEADS)[None, :],
            w_spatial[i], jnp.repeat(b_spatial[i].T, LANES, axis=1),
            ln_v_g[i][None, :], ln_v_b[i][None, :])
        o1, l1 = _attn_call(bias, qkv[:, None])
        o4, l4 = _attn_call(bias, qkv4)
        o16, l16 = _attn_call(bias, qkv16)
        h = _out_call(h, p[i], (o1[:, 0], o4, o16), (l1[:, 0], l4, l16), sg, oa, expand,
                      w_out[i].astype(BF16), g_ple[i][None, :], w_ple_gate[i].astype(BF16),
                      b_ple_gate[i][None, :], w_ple_proj[i].astype(BF16))
    return h
```

```python
import functools
import math

import jax
import jax.numpy as jnp
import numpy as np
from jax import lax
from jax.experimental import pallas as pl
from jax.experimental.pallas import tpu as pltpu

D_MODEL = 1024
N_HEADS = 8
HEAD_DIM = 64
ATTN_WIDTH = N_HEADS * HEAD_DIM
DILATIONS = ((128, 1), (512, 4), (2048, 16))
BLK = 128
ROPE_THETA = 10000.0
GMLP_GROUPS = 4
GMLP_WIDTH = D_MODEL // 2
CHUNK = 128
PLE_DIM = 256
EPS = 1e-6
NEG = -1e30
LN2 = math.log(2.0)
LOG2E = math.log2(math.e)

LANES = 128
MXU_DEPTH = 256
N_PAIRS = ATTN_WIDTH // LANES
STAT_REP = LANES // N_HEADS
VMEM_LIMIT = 56 * 1024 * 1024

PROJ_ROWS = 512
ATTN_STEP_ROWS = 2048
OUT_ROWS = 1024
PERM_ROWS = MXU_DEPTH
STRIDES = tuple(d for _, d in DILATIONS if d > 1)

BF16 = jnp.bfloat16
F32 = jnp.float32


def _dot(a, b):
    return jnp.dot(a, b, preferred_element_type=F32)


def _dot_nt(a, b):
    return lax.dot_general(a, b, (((1,), (1,)), ((), ())), preferred_element_type=F32)


def _silu(x):
    return x * jax.nn.sigmoid(x)


def _project_pieces(x_ref, gin_ref, w_ref, z_ref):
    n_rows = x_ref.shape[1]
    half = n_rows // 2
    normed = {}

    def piece(h, j):
        rows = slice(h * half, (h + 1) * half)
        if h not in normed:
            x = x_ref[0, rows, :]
            scale = lax.rsqrt(jnp.mean(x * x, axis=-1, keepdims=True) + EPS)
            normed[h] = (x * scale * gin_ref[...]).astype(BF16)
        cols = slice(j * ATTN_WIDTH, (j + 1) * ATTN_WIDTH)
        z_ref[rows, cols] = _dot(normed[h], w_ref[:, cols])

    return [(lambda h=h, j=j: piece(h, j)) for h in range(2) for j in range(w_ref.shape[1] // ATTN_WIDTH)]


def _finish_pieces(first_block, z_ref, pos_ref, invf_ref, avg_ref, qg_ref, kg_ref, ws_ref, bs_ref, lng_ref, lnb_ref,
                   d4_ref, d16_ref, qkv_ref, qkv4_ref, qkv16_ref, sg_ref, oa_ref):
    n_rows = z_ref.shape[0]
    blocks = [slice(b * CHUNK, (b + 1) * CHUNK) for b in range(n_rows // CHUNK)]
    lane = lax.broadcasted_iota(jnp.int32, (1, LANES), 1)
    first_half = (lane % HEAD_DIM) < (HEAD_DIM // 2)
    diagonal = (lax.broadcasted_iota(jnp.int32, (CHUNK, LANES), 0)
                == lax.broadcasted_iota(jnp.int32, (CHUNK, LANES), 1))
    cos, sin_signed, vn = {}, {}, {}

    def cols(j):
        return slice(j * ATTN_WIDTH, (j + 1) * ATTN_WIDTH)

    def tables(b):
        pos_row = pos_ref[0, pl.ds(first_block + b, 1), :]
        pos_col = jnp.sum(jnp.where(diagonal, pos_row, 0.0), axis=1, keepdims=True)
        ang = pos_col * invf_ref[...]
        cos[b] = jnp.cos(ang)
        sin = jnp.sin(ang)
        sin_signed[b] = jnp.where(first_half, -sin, sin)

    def norm_rope(b, j, g_ref, scale):
        z = z_ref[blocks[b], cols(j)]
        sq = (z * z).astype(BF16)
        ms = jnp.concatenate([_dot(sq[:, c:c + MXU_DEPTH], avg_ref[c:c + MXU_DEPTH, c:c + MXU_DEPTH])
                              for c in range(0, ATTN_WIDTH, MXU_DEPTH)], axis=1)
        y = z * lax.rsqrt(ms + EPS) * g_ref[...]
        outs = []
        for p in range(N_PAIRS):
            yp = y[:, p * LANES:(p + 1) * LANES]
            swapped = jnp.where(first_half, pltpu.roll(yp, LANES - HEAD_DIM // 2, 1),
                                pltpu.roll(yp, HEAD_DIM // 2, 1))
            outs.append(yp * cos[b] + swapped * sin_signed[b])
        r = jnp.concatenate(outs, axis=1)
        if scale != 1.0:
            r = r * scale
        qkv_ref[0, blocks[b], cols(j)] = r.astype(BF16)

    def value(b):
        qkv_ref[0, blocks[b], cols(2)] = z_ref[blocks[b], cols(2)].astype(BF16)

    def regroup(j, part):
        piece = qkv_ref[0, j * PERM_ROWS:(j + 1) * PERM_ROWS, cols(part)]
        for (d_ref, out_ref) in ((d4_ref, qkv4_ref), (d16_ref, qkv16_ref)):
            r = out_ref.shape[1]
            n = PERM_ROWS // r
            cm = _dot(d_ref[...], piece).astype(BF16)
            for c in range(r):
                out_ref[0, c, j * n:(j + 1) * n, cols(part)] = cm[c * n:(c + 1) * n]

    def gate(b):
        sg_ref[0, blocks[b], :] = _silu(z_ref[blocks[b], cols(3)]).astype(BF16)

    def gmlp_pre(b):
        gv = jax.nn.gelu(z_ref[blocks[b], cols(5)])
        mu = jnp.mean(gv, axis=-1, keepdims=True)
        gc = gv - mu
        var = jnp.mean(gc * gc, axis=-1, keepdims=True)
        vn[b] = (gc * lax.rsqrt(var + EPS) * lng_ref[...] + lnb_ref[...]).astype(BF16)

    def gmlp_post(b):
        r_i = lax.broadcasted_iota(jnp.int32, (CHUNK, CHUNK), 0)
        c_i = lax.broadcasted_iota(jnp.int32, (CHUNK, CHUNK), 1)
        sv = jnp.concatenate(
            [_dot(jnp.where(c_i <= r_i, ws_ref[g], 0.0).astype(BF16), vn[b][:, g * LANES:(g + 1) * LANES])
             for g in range(GMLP_GROUPS)], axis=1) + bs_ref[...]
        u = jax.nn.gelu(z_ref[blocks[b], cols(4)])
        ga = _silu(z_ref[blocks[b], cols(6)])
        oa_ref[0, blocks[b], :] = (u * sv * ga).astype(BF16)

    nb = range(len(blocks))
    nc = range(n_rows // PERM_ROWS)
    pieces = []
    for b in nb:
        pieces += [(8.0, lambda b=b: tables(b)),
                   (3.0, lambda b=b: norm_rope(b, 0, qg_ref, HEAD_DIM ** -0.5 * LOG2E)),
                   (3.0, lambda b=b: norm_rope(b, 1, kg_ref, 1.0)),
                   (0.5, lambda b=b: value(b))]
    pieces += [(1.0, lambda b=b: gate(b)) for b in nb]
    pieces += [(2.0, lambda j=j: regroup(j, 0)) for j in nc]
    pieces += [(3.5, lambda b=b: gmlp_pre(b)) for b in nb]
    pieces += [(2.0, lambda j=j: regroup(j, 1)) for j in nc]
    pieces += [(2.0, lambda j=j: regroup(j, 2)) for j in nc]
    pieces += [(4.0, lambda b=b: gmlp_post(b)) for b in nb]
    return pieces


def _interleave(weighted, uniform):
    total = sum(w for w, _ in weighted)
    out, taken, done = [], 0, 0.0
    for w, piece in weighted:
        while taken < len(uniform) and done >= (taken + 0.5) * total / len(uniform):
            out.append(uniform[taken])
            taken += 1
        out.append(piece)
        done += w
    return out + uniform[taken:]


def _proj_kernel(tiles_per_seq, x_ref, pos_ref, invf_ref, gin_ref, w_ref, *rest):
    finish_refs, (z0_ref, z1_ref) = rest[:-2], rest[-2:]
    step = pl.program_id(0)
    first_block = (jnp.maximum(step - 1, 0) % tiles_per_seq) * (x_ref.shape[1] // CHUNK)

    @pl.when(step == 0)
    def _():
        z1_ref[...] = jnp.zeros_like(z1_ref)

    def run(z_write_ref, z_read_ref):
        for piece in _interleave(_finish_pieces(first_block, z_read_ref, pos_ref, invf_ref, *finish_refs),
                                 _project_pieces(x_ref, gin_ref, w_ref, z_write_ref)):
            piece()

    @pl.when(step % 2 == 0)
    def _():
        run(z0_ref, z1_ref)

    @pl.when(step % 2 == 1)
    def _():
        run(z1_ref, z0_ref)


def _proj_call(x, pos_rows, inv_freq, g_in, w_in, avg, qg, kg, w_s, b_tab, ln_g, ln_b, perms):
    B, S, D = x.shape
    T = PROJ_ROWS
    n_cols = w_in.shape[1]
    W = ATTN_WIDTH
    per_seq = S // T
    n_tiles = B * per_seq
    proj_tile = lambda s: jnp.minimum(s, n_tiles - 1)
    done_tile = lambda s: jnp.maximum(s - 1, 0)
    tok_in = pl.BlockSpec((1, T, D), lambda s: (proj_tile(s) // per_seq, proj_tile(s) % per_seq, 0))
    tok = lambda width: pl.BlockSpec((1, T, width), lambda s: (done_tile(s) // per_seq, done_tile(s) % per_seq, 0))
    cls = lambda r: pl.BlockSpec((1, r, T // r, 3 * W),
                                 lambda s: (done_tile(s) // per_seq, 0, done_tile(s) % per_seq, 0))
    seq_pos = pl.BlockSpec((1, S // CHUNK, CHUNK), lambda s: (done_tile(s) // per_seq, 0, 0))
    const = lambda shape: pl.BlockSpec(shape, lambda s: (0,) * len(shape))
    nat_sds = lambda width: jax.ShapeDtypeStruct((B, S, width), BF16)
    cls_sds = lambda r: jax.ShapeDtypeStruct((B, r, S // r, 3 * W), BF16)
    r4, r16 = STRIDES
    return pl.pallas_call(
        functools.partial(_proj_kernel, per_seq),
        grid=(n_tiles + 1,),
        scratch_shapes=[pltpu.VMEM((T, n_cols), F32), pltpu.VMEM((T, n_cols), F32)],
        in_specs=[tok_in, seq_pos, const((1, LANES)), const((1, D)), const((D, n_cols)),
                  const((W, W)), const((1, W)), const((1, W)),
                  const((GMLP_GROUPS, CHUNK, CHUNK)), const((CHUNK, GMLP_WIDTH)),
                  const((1, GMLP_WIDTH)), const((1, GMLP_WIDTH)),
                  const((PERM_ROWS, PERM_ROWS)), const((PERM_ROWS, PERM_ROWS))],
        out_specs=[tok(3 * W), cls(r4), cls(r16), tok(W), tok(W)],
        out_shape=[nat_sds(3 * W), cls_sds(r4), cls_sds(r16), nat_sds(W), nat_sds(W)],
        compiler_params=pltpu.CompilerParams(dimension_semantics=("arbitrary",), vmem_limit_bytes=VMEM_LIMIT),
        name="proj",
    )(x, pos_rows, inv_freq, g_in, w_in, avg, qg, kg, w_s, b_tab, ln_g, ln_b, perms[r4][0], perms[r16][0])


def _attn_kernel(bias_ref, q_ref, k_ref, v_ref, kp_ref, vp_ref, o_ref, l_ref):
    n_cls, n_rows = q_ref.shape[0], q_ref.shape[1]
    first_tile = pl.program_id(2) == 0
    lane = lax.broadcasted_iota(jnp.int32, (1, LANES), 1)
    in_a = lane < HEAD_DIM
    group = lane // STAT_REP % N_PAIRS
    ones = jnp.ones((2 * BLK, LANES), BF16)
    blocks_per_cls = n_rows // BLK
    units = [(c, s, p) for c in range(n_cls) for s in range(blocks_per_cls) for p in range(N_PAIRS)]
    st = [dict() for _ in units]
    packed = {}

    def lanes_of(p):
        return slice(p * LANES, (p + 1) * LANES)

    def window(cur_ref, prev_ref, c, s, p):
        if s == 0:
            return jnp.concatenate([prev_ref[c, :, lanes_of(p)], cur_ref[c, :BLK, lanes_of(p)]], axis=0)
        return cur_ref[c, (s - 1) * BLK:(s + 1) * BLK, lanes_of(p)]

    def scores(u):
        c, s, p = units[u]
        qp = q_ref[c, s * BLK:(s + 1) * BLK, lanes_of(p)]
        zero = jnp.zeros_like(qp)
        q2 = jnp.concatenate([jnp.where(in_a, qp, zero), jnp.where(in_a, zero, qp)], axis=0)
        st[u]["sc"] = _dot_nt(q2, window(k_ref, kp_ref, c, s, p))

    def exponentials(u):
        _, s, _ = units[u]
        bias = bias_ref[first_tile.astype(jnp.int32)] if s == 0 else bias_ref[0]
        sc = st[u].pop("sc") + bias
        m = jnp.max(sc, axis=-1, keepdims=True)
        st[u]["m"] = m
        st[u]["e"] = jnp.exp2(sc - m).astype(BF16)

    def weighted_values(u):
        c, s, p = units[u]
        vv = window(v_ref, vp_ref, c, s, p)
        st[u]["res"] = _dot(st[u].pop("e"), jnp.concatenate([vv, ones], axis=1))

    def normalize(u):
        c, s, p = units[u]
        res, m = st[u].pop("res"), st[u].pop("m")
        num = jnp.where(in_a, res[:BLK, :LANES], res[BLK:, :LANES])
        den = jnp.where(in_a, res[:BLK, LANES:], res[BLK:, LANES:])
        o_ref[c, s * BLK:(s + 1) * BLK, lanes_of(p)] = (num / den).astype(o_ref.dtype)
        lse = jnp.where(in_a, m[:BLK], m[BLK:]) * LN2 + jnp.log(den)
        packed[c, s] = lse if p == 0 else jnp.where(group == p, lse, packed[c, s])
        if p == N_PAIRS - 1:
            l_ref[c, s * BLK:(s + 1) * BLK, :] = packed.pop((c, s))

    stages = (scores, exponentials, weighted_values, normalize)
    for t in range(len(units) + len(stages) - 1):
        for k, stage in enumerate(stages):
            if 0 <= t - k < len(units):
                stage(t - k)


def _attn_call(bias, qkv):
    B, r, L, W = qkv.shape[0], qkv.shape[1], qkv.shape[2], qkv.shape[3] // 3
    R = min(ATTN_STEP_ROWS, L)
    CB = min(r, ATTN_STEP_ROWS // R)
    cur = lambda width, part=0: pl.BlockSpec((None, CB, R, width), lambda b, g, i: (b, g, i, part))
    prev = lambda part: pl.BlockSpec((None, CB, BLK, W),
                                     lambda b, g, i: (b, g, jnp.maximum(i * (R // BLK) - 1, 0), part))
    return pl.pallas_call(
        _attn_kernel,
        grid=(B, r // CB, L // R),
        in_specs=[pl.BlockSpec(bias.shape, lambda b, g, i: (0, 0, 0)), cur(W, 0), cur(W, 1), cur(W, 2),
                  prev(1), prev(2)],
        out_specs=[cur(W), cur(LANES)],
        out_shape=[jax.ShapeDtypeStruct((B, r, L, W), BF16), jax.ShapeDtypeStruct((B, r, L, LANES), F32)],
        compiler_params=pltpu.CompilerParams(
            dimension_semantics=("arbitrary", "arbitrary", "arbitrary"), vmem_limit_bytes=VMEM_LIMIT),
        name=f"attn_d{r}",
    )(bias, qkv, qkv, qkv, qkv, qkv)


def _out_kernel(x_ref, p_ref, o1_ref, o4_ref, o16_ref, l1_ref, l4_ref, l16_ref, sg_ref, oa_ref,
                u4_ref, u16_ref, exp_ref, wo_ref, gple_ref, wg_ref, bg_ref, wp_ref, out_ref):
    n_blocks = x_ref.shape[1] // PERM_ROWS
    st = [dict() for _ in range(n_blocks)]

    def rows(b):
        return slice(b * PERM_ROWS, (b + 1) * PERM_ROWS)

    def grouped(ref, b):
        r = ref.shape[0]
        n = PERM_ROWS // r
        return jnp.concatenate([ref[c, b * n:(b + 1) * n, :] for c in range(r)], axis=0)

    def to_token_order(b):
        s = st[b]
        s["o4"] = _dot(u4_ref[...], grouped(o4_ref, b))
        s["o16"] = _dot(u16_ref[...], grouped(o16_ref, b))
        for name, ref, u_ref in (("l4", l4_ref, u4_ref), ("l16", l16_ref, u16_ref)):
            cm = grouped(ref, b)
            hi = cm.astype(BF16)
            lo = (cm - hi.astype(F32)).astype(BF16)
            t = _dot(u_ref[...], jnp.concatenate([hi, lo], axis=1))
            s[name] = t[:, :LANES] + t[:, LANES:]

    def merge_weights(b):
        s = st[b]
        l1 = l1_ref[0, rows(b), :]
        m = jnp.maximum(jnp.maximum(l1, s["l4"]), s["l16"])
        e1, e4, e16 = jnp.exp(l1 - m), jnp.exp(s["l4"] - m), jnp.exp(s["l16"] - m)
        inv = 1.0 / (e1 + e4 + e16)
        s["w"] = []
        for e in (e1, e4):
            w = e * inv
            w_hi = w.astype(BF16)
            w_lo = (w - w_hi.astype(F32)).astype(BF16)
            s["w"].append(jnp.concatenate([w_hi, w_lo], axis=1))

    def expand_weights(b):
        s = st[b]
        s["wf"] = [_dot(w, exp_ref[...]) for w in s["w"]]

    def merge(b):
        s = st[b]
        o1 = o1_ref[0, rows(b), :].astype(F32)
        ob = s["o16"] + s["wf"][0] * (o1 - s["o16"]) + s["wf"][1] * (s["o4"] - s["o16"])
        s["mb"] = (ob * sg_ref[0, rows(b), :].astype(F32)).astype(BF16)

    def project(b):
        s = st[b]
        s["t"] = _dot(s["mb"], wo_ref[:ATTN_WIDTH, :]) + _dot(oa_ref[0, rows(b), :], wo_ref[ATTN_WIDTH:, :])

    def residual_norm(b):
        s = st[b]
        h = x_ref[0, rows(b), :] + s["t"]
        s["h"] = h
        s["n"] = (h * lax.rsqrt(jnp.mean(h * h, axis=-1, keepdims=True) + EPS) * gple_ref[...]).astype(BF16)

    def embed_dots(b):
        s = st[b]
        s["g"] = _dot(s["n"], wg_ref[...])
        s["pp"] = _dot(p_ref[0, rows(b), :].astype(BF16), wp_ref[...])

    def embed_out(b):
        s = st[b]
        out_ref[0, rows(b), :] = s["h"] + jax.nn.sigmoid(s["g"] + bg_ref[...]) * s["pp"]

    stages = (to_token_order, merge_weights, expand_weights, merge, project, residual_norm, embed_dots, embed_out)
    for t in range(len(stages) + n_blocks - 1):
        for b in range(n_blocks):
            if 0 <= t - b < len(stages):
                stages[t - b](b)


def _out_call(x, p, os_, ls_, sg, oa, perms, expand, w_out, g_ple, w_gate, b_gate, w_proj):
    B, S, D = x.shape
    T = OUT_ROWS
    tok = lambda width: pl.BlockSpec((1, T, width), lambda b, i: (b, i, 0))
    cls = lambda r, width: pl.BlockSpec((None, r, T // r, width), lambda b, i: (b, 0, i, 0))
    const = lambda shape: pl.BlockSpec(shape, lambda b, i: (0,) * len(shape))
    r4, r16 = STRIDES
    W = ATTN_WIDTH
    return pl.pallas_call(
        _out_kernel,
        grid=(B, S // T),
        in_specs=[tok(D), tok(PLE_DIM), tok(W), cls(r4, W), cls(r16, W), tok(LANES), cls(r4, LANES),
                  cls(r16, LANES), tok(W), tok(GMLP_WIDTH),
                  const((PERM_ROWS, PERM_ROWS)), const((PERM_ROWS, PERM_ROWS)), const(expand.shape),
                  const(w_out.shape), const((1, D)), const(w_gate.shape), const((1, D)), const(w_proj.shape)],
        out_specs=tok(D),
        out_shape=jax.ShapeDtypeStruct((B, S, D), x.dtype),
        compiler_params=pltpu.CompilerParams(
            dimension_semantics=("arbitrary", "arbitrary"), vmem_limit_bytes=VMEM_LIMIT),
        name="out",
    )(x, p, *os_, *ls_, sg, oa, perms[r4][1], perms[r16][1], expand, w_out, g_ple, w_gate, b_gate, w_proj)


def _band_bias():
    qi = np.arange(BLK)[:, None]
    col = np.arange(2 * BLK)[None, :]
    dist = qi + BLK - col
    band = (dist >= 0) & (dist <= BLK)
    both = np.stack([band, band & (col >= BLK)])
    bias = np.where(both, 0.0, NEG).astype(np.float32)
    return jnp.asarray(np.concatenate([bias, bias], axis=1))


def _head_average():
    h = np.arange(ATTN_WIDTH) // HEAD_DIM
    return jnp.asarray((h[:, None] == h[None, :]).astype(np.float32) / HEAD_DIM, dtype=BF16)


def _stat_expand():
    lane = np.arange(LANES)
    g = lane // STAT_REP
    head = np.where(g < N_PAIRS, 2 * g, 2 * (g - N_PAIRS) + 1)
    pick = (lane % STAT_REP) == 0
    col_head = np.arange(ATTN_WIDTH) // HEAD_DIM
    e = ((head[:, None] == col_head[None, :]) & pick[:, None]).astype(np.float32)
    return jnp.asarray(np.concatenate([e, e], axis=0), dtype=BF16)


def _class_perms(r):
    t = np.arange(PERM_ROWS)
    grouped_pos = (t % r) * (PERM_ROWS // r) + t // r
    d = np.zeros((PERM_ROWS, PERM_ROWS), np.float32)
    d[grouped_pos, t] = 1.0
    return jnp.asarray(d, dtype=BF16), jnp.asarray(d.T, dtype=BF16)


def _inv_freq_row():
    half = HEAD_DIM // 2
    inv = jnp.exp(-math.log(ROPE_THETA) * jnp.arange(half, dtype=F32) / half)
    return jnp.tile(inv, LANES // half)[None, :]


def kernel(x, p, positions, g_in, w_in, q_norm, k_norm, w_spatial, b_spatial, ln_v_g, ln_v_b, w_out,
           g_ple, w_ple_gate, b_ple_gate, w_ple_proj):
    B, S, _ = x.shape
    depth = w_in.shape[0]
    bias = _band_bias()
    avg = _head_average()
    expand = _stat_expand()
    inv_freq = _inv_freq_row()
    perms = {r: _class_perms(r) for r in STRIDES}
    pos_rows = positions.astype(F32).reshape(B, S // CHUNK, CHUNK)
    h = x
    for i in range(depth):
        qkv, qkv4, qkv16, sg, oa = _proj_call(
            h, pos_rows, inv_freq, g_in[i][None, :], w_in[i].astype(BF16), avg,
            jnp.tile(q_norm[i], N_HEADS)[None, :], jnp.tile(k_norm[i], N_HEADS)[None, :],
            w_spatial[i], jnp.repeat(b_spatial[i].T, LANES, axis=1),
            ln_v_g[i][None, :], ln_v_b[i][None, :], perms)
        o1, l1 = _attn_call(bias, qkv[:, None])
        o4, l4 = _attn_call(bias, qkv4)
        o16, l16 = _attn_call(bias, qkv16)
        h = _out_call(h, p[i], (o1[:, 0], o4, o16), (l1[:, 0], l4, l16), sg, oa, perms, expand,
                      w_out[i].astype(BF16), g_ple[i][None, :], w_ple_gate[i].astype(BF16),
                      b_ple_gate[i][None, :], w_ple_proj[i].astype(BF16))
    return h
```

```python
import functools
import math

import jax
import jax.numpy as jnp
import numpy as np
from jax import lax
from jax.experimental import pallas as pl
from jax.experimental.pallas import tpu as pltpu

D_MODEL = 1024
N_HEADS = 8
HEAD_DIM = 64
ATTN_WIDTH = N_HEADS * HEAD_DIM
DILATIONS = ((128, 1), (512, 4), (2048, 16))
BLK = 128
ROPE_THETA = 10000.0
GMLP_GROUPS = 4
GMLP_WIDTH = D_MODEL // 2
CHUNK = 128
PLE_DIM = 256
EPS = 1e-6
NEG = -1e30
LN2 = math.log(2.0)
LOG2E = math.log2(math.e)

LANES = 128
MXU_DEPTH = 256
N_PAIRS = ATTN_WIDTH // LANES
STAT_REP = LANES // N_HEADS
VMEM_LIMIT = 56 * 1024 * 1024

PROJ_ROWS = 512
ATTN_STEP_ROWS = 2048
OUT_ROWS = 1024
PERM_ROWS = MXU_DEPTH
STRIDES = tuple(d for _, d in DILATIONS if d > 1)

BF16 = jnp.bfloat16
F32 = jnp.float32


def _dot(a, b):
    return jnp.dot(a, b, preferred_element_type=F32)


def _dot_nt(a, b):
    return lax.dot_general(a, b, (((1,), (1,)), ((), ())), preferred_element_type=F32)


def _silu(x):
    return x * jax.nn.sigmoid(x)


def _project_pieces(x_ref, gin_ref, w_ref, z_ref):
    n_rows = x_ref.shape[1]
    half = n_rows // 2
    normed = {}

    def piece(h, j):
        rows = slice(h * half, (h + 1) * half)
        if h not in normed:
            x = x_ref[0, rows, :]
            scale = lax.rsqrt(jnp.mean(x * x, axis=-1, keepdims=True) + EPS)
            normed[h] = (x * scale * gin_ref[...]).astype(BF16)
        cols = slice(j * ATTN_WIDTH, (j + 1) * ATTN_WIDTH)
        z_ref[rows, cols] = _dot(normed[h], w_ref[:, cols])

    return [(lambda h=h, j=j: piece(h, j)) for h in range(2) for j in range(w_ref.shape[1] // ATTN_WIDTH)]


def _finish_pieces(first_block, z_ref, pos_ref, invf_ref, avg_ref, qg_ref, kg_ref, ws_ref, bs_ref, lng_ref, lnb_ref,
                   d4_ref, d16_ref, qkv_ref, qkv4_ref, qkv16_ref, sg_ref, oa_ref):
    n_rows = z_ref.shape[0]
    blocks = [slice(b * CHUNK, (b + 1) * CHUNK) for b in range(n_rows // CHUNK)]
    lane = lax.broadcasted_iota(jnp.int32, (1, LANES), 1)
    first_half = (lane % HEAD_DIM) < (HEAD_DIM // 2)
    cos, sin_signed, vn = {}, {}, {}

    def cols(j):
        return slice(j * ATTN_WIDTH, (j + 1) * ATTN_WIDTH)

    def tables(b):
        pos_row = pos_ref[0, pl.ds(first_block + b, 1), :]
        ang = pos_row * invf_ref[...]
        to_rows = lambda t: jnp.concatenate([t] * (LANES // t.shape[0]), axis=0).T
        cos[b] = to_rows(jnp.cos(ang))
        sin = to_rows(jnp.sin(ang))
        sin_signed[b] = jnp.where(first_half, -sin, sin)

    def norm_rope(b, j, g_ref, scale):
        z = z_ref[blocks[b], cols(j)]
        sq = (z * z).astype(BF16)
        ms = jnp.concatenate([_dot(sq[:, c:c + MXU_DEPTH], avg_ref[c:c + MXU_DEPTH, c:c + MXU_DEPTH])
                              for c in range(0, ATTN_WIDTH, MXU_DEPTH)], axis=1)
        y = z * lax.rsqrt(ms + EPS) * g_ref[...]
        outs = []
        for p in range(N_PAIRS):
            yp = y[:, p * LANES:(p + 1) * LANES]
            swapped = jnp.where(first_half, pltpu.roll(yp, LANES - HEAD_DIM // 2, 1),
                                pltpu.roll(yp, HEAD_DIM // 2, 1))
            outs.append(yp * cos[b] + swapped * sin_signed[b])
        r = jnp.concatenate(outs, axis=1)
        if scale != 1.0:
            r = r * scale
        qkv_ref[0, blocks[b], cols(j)] = r.astype(BF16)

    def value(b):
        qkv_ref[0, blocks[b], cols(2)] = z_ref[blocks[b], cols(2)].astype(BF16)

    def regroup(j, part):
        piece = qkv_ref[0, j * PERM_ROWS:(j + 1) * PERM_ROWS, cols(part)]
        for (d_ref, out_ref) in ((d4_ref, qkv4_ref), (d16_ref, qkv16_ref)):
            r = out_ref.shape[1]
            n = PERM_ROWS // r
            cm = _dot(d_ref[...], piece).astype(BF16)
            for c in range(r):
                out_ref[0, c, j * n:(j + 1) * n, cols(part)] = cm[c * n:(c + 1) * n]

    def gate(b):
        sg_ref[0, blocks[b], :] = _silu(z_ref[blocks[b], cols(3)]).astype(BF16)

    def gmlp_pre(b):
        gv = jax.nn.gelu(z_ref[blocks[b], cols(5)])
        mu = jnp.mean(gv, axis=-1, keepdims=True)
        gc = gv - mu
        var = jnp.mean(gc * gc, axis=-1, keepdims=True)
        vn[b] = (gc * lax.rsqrt(var + EPS) * lng_ref[...] + lnb_ref[...]).astype(BF16)

    def gmlp_post(b):
        r_i = lax.broadcasted_iota(jnp.int32, (CHUNK, CHUNK), 0)
        c_i = lax.broadcasted_iota(jnp.int32, (CHUNK, CHUNK), 1)
        sv = jnp.concatenate(
            [_dot(jnp.where(c_i <= r_i, ws_ref[g], 0.0).astype(BF16), vn[b][:, g * LANES:(g + 1) * LANES])
             for g in range(GMLP_GROUPS)], axis=1) + bs_ref[...]
        u = jax.nn.gelu(z_ref[blocks[b], cols(4)])
        ga = _silu(z_ref[blocks[b], cols(6)])
        oa_ref[0, blocks[b], :] = (u * sv * ga).astype(BF16)

    nb = range(len(blocks))
    nc = range(n_rows // PERM_ROWS)
    pieces = []
    for b in nb:
        pieces += [(2.5, lambda b=b: tables(b)),
                   (3.0, lambda b=b: norm_rope(b, 0, qg_ref, HEAD_DIM ** -0.5 * LOG2E)),
                   (3.0, lambda b=b: norm_rope(b, 1, kg_ref, 1.0)),
                   (0.5, lambda b=b: value(b))]
    pieces += [(1.0, lambda b=b: gate(b)) for b in nb]
    pieces += [(2.0, lambda j=j: regroup(j, 0)) for j in nc]
    pieces += [(3.5, lambda b=b: gmlp_pre(b)) for b in nb]
    pieces += [(2.0, lambda j=j: regroup(j, 1)) for j in nc]
    pieces += [(2.0, lambda j=j: regroup(j, 2)) for j in nc]
    pieces += [(4.0, lambda b=b: gmlp_post(b)) for b in nb]
    return pieces


def _interleave(weighted, uniform):
    total = sum(w for w, _ in weighted)
    out, taken, done = [], 0, 0.0
    for w, piece in weighted:
        while taken < len(uniform) and done >= (taken + 0.5) * total / len(uniform):
            out.append(uniform[taken])
            taken += 1
        out.append(piece)
        done += w
    return out + uniform[taken:]


def _proj_kernel(tiles_per_seq, x_ref, pos_ref, invf_ref, gin_ref, w_ref, *rest):
    finish_refs, (z0_ref, z1_ref) = rest[:-2], rest[-2:]
    step = pl.program_id(0)
    first_block = (jnp.maximum(step - 1, 0) % tiles_per_seq) * (x_ref.shape[1] // CHUNK)

    @pl.when(step == 0)
    def _():
        z1_ref[...] = jnp.zeros_like(z1_ref)

    def run(z_write_ref, z_read_ref):
        for piece in _interleave(_finish_pieces(first_block, z_read_ref, pos_ref, invf_ref, *finish_refs),
                                 _project_pieces(x_ref, gin_ref, w_ref, z_write_ref)):
            piece()

    @pl.when(step % 2 == 0)
    def _():
        run(z0_ref, z1_ref)

    @pl.when(step % 2 == 1)
    def _():
        run(z1_ref, z0_ref)


def _proj_call(x, pos_rows, inv_freq, g_in, w_in, avg, qg, kg, w_s, b_tab, ln_g, ln_b, perms):
    B, S, D = x.shape
    T = PROJ_ROWS
    n_cols = w_in.shape[1]
    W = ATTN_WIDTH
    per_seq = S // T
    n_tiles = B * per_seq
    proj_tile = lambda s: jnp.minimum(s, n_tiles - 1)
    done_tile = lambda s: jnp.maximum(s - 1, 0)
    tok_in = pl.BlockSpec((1, T, D), lambda s: (proj_tile(s) // per_seq, proj_tile(s) % per_seq, 0))
    tok = lambda width: pl.BlockSpec((1, T, width), lambda s: (done_tile(s) // per_seq, done_tile(s) % per_seq, 0))
    cls = lambda r: pl.BlockSpec((1, r, T // r, 3 * W),
                                 lambda s: (done_tile(s) // per_seq, 0, done_tile(s) % per_seq, 0))
    seq_pos = pl.BlockSpec((1, S // CHUNK, CHUNK), lambda s: (done_tile(s) // per_seq, 0, 0))
    const = lambda shape: pl.BlockSpec(shape, lambda s: (0,) * len(shape))
    nat_sds = lambda width: jax.ShapeDtypeStruct((B, S, width), BF16)
    cls_sds = lambda r: jax.ShapeDtypeStruct((B, r, S // r, 3 * W), BF16)
    r4, r16 = STRIDES
    return pl.pallas_call(
        functools.partial(_proj_kernel, per_seq),
        grid=(n_tiles + 1,),
        scratch_shapes=[pltpu.VMEM((T, n_cols), F32), pltpu.VMEM((T, n_cols), F32)],
        in_specs=[tok_in, seq_pos, const(inv_freq.shape), const((1, D)), const((D, n_cols)),
                  const((W, W)), const((1, W)), const((1, W)),
                  const((GMLP_GROUPS, CHUNK, CHUNK)), const((CHUNK, GMLP_WIDTH)),
                  const((1, GMLP_WIDTH)), const((1, GMLP_WIDTH)),
                  const((PERM_ROWS, PERM_ROWS)), const((PERM_ROWS, PERM_ROWS))],
        out_specs=[tok(3 * W), cls(r4), cls(r16), tok(W), tok(W)],
        out_shape=[nat_sds(3 * W), cls_sds(r4), cls_sds(r16), nat_sds(W), nat_sds(W)],
        compiler_params=pltpu.CompilerParams(dimension_semantics=("arbitrary",), vmem_limit_bytes=VMEM_LIMIT),
        name="proj",
    )(x, pos_rows, inv_freq, g_in, w_in, avg, qg, kg, w_s, b_tab, ln_g, ln_b, perms[r4][0], perms[r16][0])


def _attn_kernel(bias_ref, q_ref, k_ref, v_ref, kp_ref, vp_ref, o_ref, l_ref):
    n_cls, n_rows = q_ref.shape[0], q_ref.shape[1]
    first_tile = pl.program_id(2) == 0
    lane = lax.broadcasted_iota(jnp.int32, (1, LANES), 1)
    in_a = lane < HEAD_DIM
    group = lane // STAT_REP % N_PAIRS
    ones = jnp.ones((2 * BLK, LANES), BF16)
    blocks_per_cls = n_rows // BLK
    units = [(c, s, p) for c in range(n_cls) for s in range(blocks_per_cls) for p in range(N_PAIRS)]
    st = [dict() for _ in units]
    packed = {}

    def lanes_of(p):
        return slice(p * LANES, (p + 1) * LANES)

    def window(cur_ref, prev_ref, c, s, p):
        if s == 0:
            return jnp.concatenate([prev_ref[c, :, lanes_of(p)], cur_ref[c, :BLK, lanes_of(p)]], axis=0)
        return cur_ref[c, (s - 1) * BLK:(s + 1) * BLK, lanes_of(p)]

    def scores(u):
        c, s, p = units[u]
        qp = q_ref[c, s * BLK:(s + 1) * BLK, lanes_of(p)]
        zero = jnp.zeros_like(qp)
        q2 = jnp.concatenate([jnp.where(in_a, qp, zero), jnp.where(in_a, zero, qp)], axis=0)
        st[u]["sc"] = _dot_nt(q2, window(k_ref, kp_ref, c, s, p))

    def exponentials(u):
        _, s, _ = units[u]
        bias = bias_ref[first_tile.astype(jnp.int32)] if s == 0 else bias_ref[0]
        sc = st[u].pop("sc") + bias
        m = jnp.max(sc, axis=-1, keepdims=True)
        st[u]["m"] = m
        st[u]["e"] = jnp.exp2(sc - m).astype(BF16)

    def weighted_values(u):
        c, s, p = units[u]
        vv = window(v_ref, vp_ref, c, s, p)
        st[u]["res"] = _dot(st[u].pop("e"), jnp.concatenate([vv, ones], axis=1))

    def normalize(u):
        c, s, p = units[u]
        res, m = st[u].pop("res"), st[u].pop("m")
        num = jnp.where(in_a, res[:BLK, :LANES], res[BLK:, :LANES])
        den = jnp.where(in_a, res[:BLK, LANES:], res[BLK:, LANES:])
        o_ref[c, s * BLK:(s + 1) * BLK, lanes_of(p)] = (num / den).astype(o_ref.dtype)
        lse = jnp.where(in_a, m[:BLK], m[BLK:]) * LN2 + jnp.log(den)
        packed[c, s] = lse if p == 0 else jnp.where(group == p, lse, packed[c, s])
        if p == N_PAIRS - 1:
            l_ref[c, s * BLK:(s + 1) * BLK, :] = packed.pop((c, s))

    stages = (scores, exponentials, weighted_values, normalize)
    for t in range(len(units) + len(stages) - 1):
        for k, stage in enumerate(stages):
            if 0 <= t - k < len(units):
                stage(t - k)


def _attn_call(bias, qkv):
    B, r, L, W = qkv.shape[0], qkv.shape[1], qkv.shape[2], qkv.shape[3] // 3
    R = min(ATTN_STEP_ROWS, L)
    CB = min(r, ATTN_STEP_ROWS // R)
    cur = lambda width, part=0: pl.BlockSpec((None, CB, R, width), lambda b, g, i: (b, g, i, part))
    prev = lambda part: pl.BlockSpec((None, CB, BLK, W),
                                     lambda b, g, i: (b, g, jnp.maximum(i * (R // BLK) - 1, 0), part))
    return pl.pallas_call(
        _attn_kernel,
        grid=(B, r // CB, L // R),
        in_specs=[pl.BlockSpec(bias.shape, lambda b, g, i: (0, 0, 0)), cur(W, 0), cur(W, 1), cur(W, 2),
                  prev(1), prev(2)],
        out_specs=[cur(W), cur(LANES)],
        out_shape=[jax.ShapeDtypeStruct((B, r, L, W), BF16), jax.ShapeDtypeStruct((B, r, L, LANES), F32)],
        compiler_params=pltpu.CompilerParams(
            dimension_semantics=("arbitrary", "arbitrary", "arbitrary"), vmem_limit_bytes=VMEM_LIMIT),
        name=f"attn_d{r}",
    )(bias, qkv, qkv, qkv, qkv, qkv)


def _out_kernel(x_ref, p_ref, o1_ref, o4_ref, o16_ref, l1_ref, l4_ref, l16_ref, sg_ref, oa_ref,
                u4_ref, u16_ref, wo_ref, gple_ref, wg_ref, bg_ref, wp_ref, out_ref):
    n_blocks = x_ref.shape[1] // PERM_ROWS
    st = [dict() for _ in range(n_blocks)]

    def rows(b):
        return slice(b * PERM_ROWS, (b + 1) * PERM_ROWS)

    def grouped(ref, b):
        r = ref.shape[0]
        n = PERM_ROWS // r
        return jnp.concatenate([ref[c, b * n:(b + 1) * n, :] for c in range(r)], axis=0)

    def to_token_order(b):
        s = st[b]
        s["o4"] = _dot(u4_ref[...], grouped(o4_ref, b))
        s["o16"] = _dot(u16_ref[...], grouped(o16_ref, b))
        for name, ref, u_ref in (("l4", l4_ref, u4_ref), ("l16", l16_ref, u16_ref)):
            cm = grouped(ref, b)
            hi = cm.astype(BF16)
            lo = (cm - hi.astype(F32)).astype(BF16)
            t = _dot(u_ref[...], jnp.concatenate([hi, lo], axis=1))
            s[name] = t[:, :LANES] + t[:, LANES:]

    def merge_weights(b):
        s = st[b]
        l1 = l1_ref[0, rows(b), :]
        m = jnp.maximum(jnp.maximum(l1, s["l4"]), s["l16"])
        e1, e4, e16 = jnp.exp(l1 - m), jnp.exp(s["l4"] - m), jnp.exp(s["l16"] - m)
        inv = 1.0 / (e1 + e4 + e16)
        s["w"] = [e1 * inv, e4 * inv]

    lane = lax.broadcasted_iota(jnp.int32, (1, LANES), 1)

    def spread(w):
        pairs = []
        for p in range(N_PAIRS):
            t = w if p == 0 else pltpu.roll(w, LANES - STAT_REP * p, 1)
            for width in (STAT_REP, 2 * STAT_REP):
                t = jnp.where(lane % (2 * width) < width, t, pltpu.roll(t, width, 1))
            pairs.append(t)
        return jnp.concatenate(pairs, axis=1)

    def expand_weights(b):
        s = st[b]
        s["wf"] = [spread(w) for w in s["w"]]

    def merge(b):
        s = st[b]
        o1 = o1_ref[0, rows(b), :].astype(F32)
        ob = s["o16"] + s["wf"][0] * (o1 - s["o16"]) + s["wf"][1] * (s["o4"] - s["o16"])
        s["mb"] = (ob * sg_ref[0, rows(b), :].astype(F32)).astype(BF16)

    def project(b):
        s = st[b]
        s["t"] = _dot(s["mb"], wo_ref[:ATTN_WIDTH, :]) + _dot(oa_ref[0, rows(b), :], wo_ref[ATTN_WIDTH:, :])

    def residual_norm(b):
        s = st[b]
        h = x_ref[0, rows(b), :] + s["t"]
        s["h"] = h
        s["n"] = (h * lax.rsqrt(jnp.mean(h * h, axis=-1, keepdims=True) + EPS) * gple_ref[...]).astype(BF16)

    def embed_dots(b):
        s = st[b]
        s["g"] = _dot(s["n"], wg_ref[...])
        s["pp"] = _dot(p_ref[0, rows(b), :].astype(BF16), wp_ref[...])

    def embed_out(b):
        s = st[b]
        out_ref[0, rows(b), :] = s["h"] + jax.nn.sigmoid(s["g"] + bg_ref[...]) * s["pp"]

    stages = (to_token_order, merge_weights, expand_weights, merge, project, residual_norm, embed_dots, embed_out)
    for t in range(len(stages) + n_blocks - 1):
        for b in range(n_blocks):
            if 0 <= t - b < len(stages):
                stages[t - b](b)


def _out_call(x, p, os_, ls_, sg, oa, perms, w_out, g_ple, w_gate, b_gate, w_proj):
    B, S, D = x.shape
    T = OUT_ROWS
    tok = lambda width: pl.BlockSpec((1, T, width), lambda b, i: (b, i, 0))
    cls = lambda r, width: pl.BlockSpec((None, r, T // r, width), lambda b, i: (b, 0, i, 0))
    const = lambda shape: pl.BlockSpec(shape, lambda b, i: (0,) * len(shape))
    r4, r16 = STRIDES
    W = ATTN_WIDTH
    return pl.pallas_call(
        _out_kernel,
        grid=(B, S // T),
        in_specs=[tok(D), tok(PLE_DIM), tok(W), cls(r4, W), cls(r16, W), tok(LANES), cls(r4, LANES),
                  cls(r16, LANES), tok(W), tok(GMLP_WIDTH),
                  const((PERM_ROWS, PERM_ROWS)), const((PERM_ROWS, PERM_ROWS)),
                  const(w_out.shape), const((1, D)), const(w_gate.shape), const((1, D)), const(w_proj.shape)],
        out_specs=tok(D),
        out_shape=jax.ShapeDtypeStruct((B, S, D), x.dtype),
        compiler_params=pltpu.CompilerParams(
            dimension_semantics=("arbitrary", "arbitrary"), vmem_limit_bytes=VMEM_LIMIT),
        name="out",
    )(x, p, *os_, *ls_, sg, oa, perms[r4][1], perms[r16][1], w_out, g_ple, w_gate, b_gate, w_proj)


def _band_bias():
    qi = np.arange(BLK)[:, None]
    col = np.arange(2 * BLK)[None, :]
    dist = qi + BLK - col
    band = (dist >= 0) & (dist <= BLK)
    both = np.stack([band, band & (col >= BLK)])
    bias = np.where(both, 0.0, NEG).astype(np.float32)
    return jnp.asarray(np.concatenate([bias, bias], axis=1))


def _head_average():
    h = np.arange(ATTN_WIDTH) // HEAD_DIM
    return jnp.asarray((h[:, None] == h[None, :]).astype(np.float32) / HEAD_DIM, dtype=BF16)


def _class_perms(r):
    t = np.arange(PERM_ROWS)
    grouped_pos = (t % r) * (PERM_ROWS // r) + t // r
    d = np.zeros((PERM_ROWS, PERM_ROWS), np.float32)
    d[grouped_pos, t] = 1.0
    return jnp.asarray(d, dtype=BF16), jnp.asarray(d.T, dtype=BF16)


def _inv_freq_col():
    half = HEAD_DIM // 2
    return jnp.exp(-math.log(ROPE_THETA) * jnp.arange(half, dtype=F32) / half)[:, None]


def kernel(x, p, positions, g_in, w_in, q_norm, k_norm, w_spatial, b_spatial, ln_v_g, ln_v_b, w_out,
           g_ple, w_ple_gate, b_ple_gate, w_ple_proj):
    B, S, _ = x.shape
    depth = w_in.shape[0]
    bias = _band_bias()
    avg = _head_average()
    inv_freq = _inv_freq_col()
    perms = {r: _class_perms(r) for r in STRIDES}
    pos_rows = positions.astype(F32).reshape(B, S // CHUNK, CHUNK)
    h = x
    for i in range(depth):
        qkv, qkv4, qkv16, sg, oa = _proj_call(
            h, pos_rows, inv_freq, g_in[i][None, :], w_in[i].astype(BF16), avg,
            jnp.tile(q_norm[i], N_HEADS)[None, :], jnp.tile(k_norm[i], N_HEADS)[None, :],
            w_spatial[i], jnp.repeat(b_spatial[i].T, LANES, axis=1),
            ln_v_g[i][None, :], ln_v_b[i][None, :], perms)
        o1, l1 = _attn_call(bias, qkv[:, None])
        o4, l4 = _attn_call(bias, qkv4)
        o16, l16 = _attn_call(bias, qkv16)
        h = _out_call(h, p[i], (o1[:, 0], o4, o16), (l1[:, 0], l4, l16), sg, oa, perms,
                      w_out[i].astype(BF16), g_ple[i][None, :], w_ple_gate[i].astype(BF16),
                      b_ple_gate[i][None, :], w_ple_proj[i].astype(BF16))
    return h
```

```python
import functools
import math

import jax
import jax.numpy as jnp
import numpy as np
from jax import lax
from jax.experimental import pallas as pl
from jax.experimental.pallas import tpu as pltpu

D_MODEL = 1024
N_HEADS = 8
HEAD_DIM = 64
ATTN_WIDTH = N_HEADS * HEAD_DIM
DILATIONS = ((128, 1), (512, 4), (2048, 16))
BLK = 128
ROPE_THETA = 10000.0
GMLP_GROUPS = 4
GMLP_WIDTH = D_MODEL // 2
CHUNK = 128
PLE_DIM = 256
EPS = 1e-6
NEG = -1e30
LN2 = math.log(2.0)
LOG2E = math.log2(math.e)

LANES = 128
MXU_DEPTH = 256
N_PAIRS = ATTN_WIDTH // LANES
STAT_REP = LANES // N_HEADS
VMEM_LIMIT = 56 * 1024 * 1024

PROJ_ROWS = 512
ATTN_STEP_ROWS = 2048
OUT_ROWS = 1024
PERM_ROWS = MXU_DEPTH
STRIDES = tuple(d for _, d in DILATIONS if d > 1)

BF16 = jnp.bfloat16
F32 = jnp.float32


def _dot(a, b):
    return jnp.dot(a, b, preferred_element_type=F32)


def _dot_nt(a, b):
    return lax.dot_general(a, b, (((1,), (1,)), ((), ())), preferred_element_type=F32)


def _silu(x):
    return x * jax.nn.sigmoid(x)


def _project_pieces(x_ref, gin_ref, w_ref, z_ref):
    n_rows = x_ref.shape[1]
    half = n_rows // 2
    normed = {}

    def piece(h, j):
        rows = slice(h * half, (h + 1) * half)
        if h not in normed:
            x = x_ref[0, rows, :]
            scale = lax.rsqrt(jnp.mean(x * x, axis=-1, keepdims=True) + EPS)
            normed[h] = (x * scale * gin_ref[...]).astype(BF16)
        cols = slice(j * ATTN_WIDTH, (j + 1) * ATTN_WIDTH)
        z_ref[rows, cols] = _dot(normed[h], w_ref[:, cols])

    return [(lambda h=h, j=j: piece(h, j)) for h in range(2) for j in range(w_ref.shape[1] // ATTN_WIDTH)]


def _finish_pieces(first_block, z_ref, pos_ref, invf_ref, avg_ref, qg_ref, kg_ref, ws_ref, bs_ref, lng_ref, lnb_ref,
                   d4_ref, d16_ref, qkv_ref, qkv4_ref, qkv16_ref, sg_ref, oa_ref):
    n_rows = z_ref.shape[0]
    blocks = [slice(b * CHUNK, (b + 1) * CHUNK) for b in range(n_rows // CHUNK)]
    lane = lax.broadcasted_iota(jnp.int32, (1, LANES), 1)
    first_half = (lane % HEAD_DIM) < (HEAD_DIM // 2)
    cos, sin_signed, vn = {}, {}, {}

    def cols(j):
        return slice(j * ATTN_WIDTH, (j + 1) * ATTN_WIDTH)

    def tables(b):
        pos_row = pos_ref[0, pl.ds(first_block + b, 1), :]
        ang = pos_row * invf_ref[...]
        to_rows = lambda t: jnp.concatenate([t] * (LANES // t.shape[0]), axis=0).T
        cos[b] = to_rows(jnp.cos(ang))
        sin = to_rows(jnp.sin(ang))
        sin_signed[b] = jnp.where(first_half, -sin, sin)

    def norm_rope(b):
        zq, zk = z_ref[blocks[b], cols(0)], z_ref[blocks[b], cols(1)]
        sq = jnp.concatenate([(zq * zq).astype(BF16), (zk * zk).astype(BF16)], axis=0)
        ms = jnp.concatenate([_dot(sq[:, c:c + MXU_DEPTH], avg_ref[c:c + MXU_DEPTH, c:c + MXU_DEPTH])
                              for c in range(0, ATTN_WIDTH, MXU_DEPTH)], axis=1)
        for j, (z, g_ref, scale) in enumerate(((zq, qg_ref, HEAD_DIM ** -0.5 * LOG2E), (zk, kg_ref, 1.0))):
            y = z * lax.rsqrt(ms[j * CHUNK:(j + 1) * CHUNK] + EPS) * g_ref[...]
            outs = []
            for p in range(N_PAIRS):
                yp = y[:, p * LANES:(p + 1) * LANES]
                swapped = jnp.where(first_half, pltpu.roll(yp, LANES - HEAD_DIM // 2, 1),
                                    pltpu.roll(yp, HEAD_DIM // 2, 1))
                outs.append(yp * cos[b] + swapped * sin_signed[b])
            r = jnp.concatenate(outs, axis=1)
            if scale != 1.0:
                r = r * scale
            qkv_ref[0, blocks[b], cols(j)] = r.astype(BF16)

    def value(b):
        qkv_ref[0, blocks[b], cols(2)] = z_ref[blocks[b], cols(2)].astype(BF16)

    def regroup(j, part):
        piece = qkv_ref[0, j * PERM_ROWS:(j + 1) * PERM_ROWS, cols(part)]
        for (d_ref, out_ref) in ((d4_ref, qkv4_ref), (d16_ref, qkv16_ref)):
            r = out_ref.shape[1]
            n = PERM_ROWS // r
            cm = _dot(d_ref[...], piece).astype(BF16)
            for c in range(r):
                out_ref[0, c, j * n:(j + 1) * n, cols(part)] = cm[c * n:(c + 1) * n]

    def gate(b):
        sg_ref[0, blocks[b], :] = _silu(z_ref[blocks[b], cols(3)]).astype(BF16)

    def gmlp_pre(b):
        gv = jax.nn.gelu(z_ref[blocks[b], cols(5)])
        mu = jnp.mean(gv, axis=-1, keepdims=True)
        gc = gv - mu
        var = jnp.mean(gc * gc, axis=-1, keepdims=True)
        vn[b] = (gc * lax.rsqrt(var + EPS) * lng_ref[...] + lnb_ref[...]).astype(BF16)

    def gmlp_post(b):
        r_i = lax.broadcasted_iota(jnp.int32, (CHUNK, CHUNK), 0)
        c_i = lax.broadcasted_iota(jnp.int32, (CHUNK, CHUNK), 1)
        sv = jnp.concatenate(
            [_dot(jnp.where(c_i <= r_i, ws_ref[g], 0.0).astype(BF16), vn[b][:, g * LANES:(g + 1) * LANES])
             for g in range(GMLP_GROUPS)], axis=1) + bs_ref[...]
        u = jax.nn.gelu(z_ref[blocks[b], cols(4)])
        ga = _silu(z_ref[blocks[b], cols(6)])
        oa_ref[0, blocks[b], :] = (u * sv * ga).astype(BF16)

    nb = range(len(blocks))
    nc = range(n_rows // PERM_ROWS)
    pieces = []
    for b in nb:
        pieces += [(2.5, lambda b=b: tables(b)),
                   (6.0, lambda b=b: norm_rope(b)),
                   (0.5, lambda b=b: value(b))]
    pieces += [(1.0, lambda b=b: gate(b)) for b in nb]
    pieces += [(2.0, lambda j=j: regroup(j, 0)) for j in nc]
    pieces += [(3.5, lambda b=b: gmlp_pre(b)) for b in nb]
    pieces += [(2.0, lambda j=j: regroup(j, 1)) for j in nc]
    pieces += [(2.0, lambda j=j: regroup(j, 2)) for j in nc]
    pieces += [(4.0, lambda b=b: gmlp_post(b)) for b in nb]
    return pieces


def _interleave(weighted, uniform):
    total = sum(w for w, _ in weighted)
    out, taken, done = [], 0, 0.0
    for w, piece in weighted:
        while taken < len(uniform) and done >= (taken + 0.5) * total / len(uniform):
            out.append(uniform[taken])
            taken += 1
        out.append(piece)
        done += w
    return out + uniform[taken:]


def _proj_kernel(tiles_per_seq, x_ref, pos_ref, invf_ref, gin_ref, w_ref, *rest):
    finish_refs, (z0_ref, z1_ref) = rest[:-2], rest[-2:]
    step = pl.program_id(0)
    first_block = (jnp.maximum(step - 1, 0) % tiles_per_seq) * (x_ref.shape[1] // CHUNK)

    @pl.when(step == 0)
    def _():
        z1_ref[...] = jnp.zeros_like(z1_ref)

    def run(z_write_ref, z_read_ref):
        for piece in _interleave(_finish_pieces(first_block, z_read_ref, pos_ref, invf_ref, *finish_refs),
                                 _project_pieces(x_ref, gin_ref, w_ref, z_write_ref)):
            piece()

    @pl.when(step % 2 == 0)
    def _():
        run(z0_ref, z1_ref)

    @pl.when(step % 2 == 1)
    def _():
        run(z1_ref, z0_ref)


def _proj_call(x, pos_rows, inv_freq, g_in, w_in, avg, qg, kg, w_s, b_tab, ln_g, ln_b, perms):
    B, S, D = x.shape
    T = PROJ_ROWS
    n_cols = w_in.shape[1]
    W = ATTN_WIDTH
    per_seq = S // T
    n_tiles = B * per_seq
    proj_tile = lambda s: jnp.minimum(s, n_tiles - 1)
    done_tile = lambda s: jnp.maximum(s - 1, 0)
    tok_in = pl.BlockSpec((1, T, D), lambda s: (proj_tile(s) // per_seq, proj_tile(s) % per_seq, 0))
    tok = lambda width: pl.BlockSpec((1, T, width), lambda s: (done_tile(s) // per_seq, done_tile(s) % per_seq, 0))
    cls = lambda r: pl.BlockSpec((1, r, T // r, 3 * W),
                                 lambda s: (done_tile(s) // per_seq, 0, done_tile(s) % per_seq, 0))
    seq_pos = pl.BlockSpec((1, S // CHUNK, CHUNK), lambda s: (done_tile(s) // per_seq, 0, 0))
    const = lambda shape: pl.BlockSpec(shape, lambda s: (0,) * len(shape))
    nat_sds = lambda width: jax.ShapeDtypeStruct((B, S, width), BF16)
    cls_sds = lambda r: jax.ShapeDtypeStruct((B, r, S // r, 3 * W), BF16)
    r4, r16 = STRIDES
    return pl.pallas_call(
        functools.partial(_proj_kernel, per_seq),
        grid=(n_tiles + 1,),
        scratch_shapes=[pltpu.VMEM((T, n_cols), F32), pltpu.VMEM((T, n_cols), F32)],
        in_specs=[tok_in, seq_pos, const(inv_freq.shape), const((1, D)), const((D, n_cols)),
                  const((W, W)), const((1, W)), const((1, W)),
                  const((GMLP_GROUPS, CHUNK, CHUNK)), const((CHUNK, GMLP_WIDTH)),
                  const((1, GMLP_WIDTH)), const((1, GMLP_WIDTH)),
                  const((PERM_ROWS, PERM_ROWS)), const((PERM_ROWS, PERM_ROWS))],
        out_specs=[tok(3 * W), cls(r4), cls(r16), tok(W), tok(W)],
        out_shape=[nat_sds(3 * W), cls_sds(r4), cls_sds(r16), nat_sds(W), nat_sds(W)],
        compiler_params=pltpu.CompilerParams(dimension_semantics=("arbitrary",), vmem_limit_bytes=VMEM_LIMIT),
        name="proj",
    )(x, pos_rows, inv_freq, g_in, w_in, avg, qg, kg, w_s, b_tab, ln_g, ln_b, perms[r4][0], perms[r16][0])


def _attn_kernel(bias_ref, q_ref, k_ref, v_ref, kp_ref, vp_ref, o_ref, l_ref):
    n_cls, n_rows = q_ref.shape[0], q_ref.shape[1]
    first_tile = pl.program_id(2) == 0
    lane = lax.broadcasted_iota(jnp.int32, (1, LANES), 1)
    in_a = lane < HEAD_DIM
    group = lane // STAT_REP % N_PAIRS
    ones = jnp.ones((2 * BLK, LANES), BF16)
    blocks_per_cls = n_rows // BLK
    units = [(c, s, p) for c in range(n_cls) for s in range(blocks_per_cls) for p in range(N_PAIRS)]
    st = [dict() for _ in units]
    packed = {}

    def lanes_of(p):
        return slice(p * LANES, (p + 1) * LANES)

    def window(cur_ref, prev_ref, c, s, p):
        if s == 0:
            return jnp.concatenate([prev_ref[c, :, lanes_of(p)], cur_ref[c, :BLK, lanes_of(p)]], axis=0)
        return cur_ref[c, (s - 1) * BLK:(s + 1) * BLK, lanes_of(p)]

    def scores(u):
        c, s, p = units[u]
        qp = q_ref[c, s * BLK:(s + 1) * BLK, lanes_of(p)]
        zero = jnp.zeros_like(qp)
        q2 = jnp.concatenate([jnp.where(in_a, qp, zero), jnp.where(in_a, zero, qp)], axis=0)
        st[u]["sc"] = _dot_nt(q2, window(k_ref, kp_ref, c, s, p))

    def exponentials(u):
        _, s, _ = units[u]
        bias = bias_ref[first_tile.astype(jnp.int32)] if s == 0 else bias_ref[0]
        sc = st[u].pop("sc") + bias
        m = jnp.max(sc, axis=-1, keepdims=True)
        st[u]["m"] = m
        st[u]["e"] = jnp.exp2(sc - m).astype(BF16)

    def weighted_values(u):
        c, s, p = units[u]
        vv = window(v_ref, vp_ref, c, s, p)
        st[u]["res"] = _dot(st[u].pop("e"), jnp.concatenate([vv, ones], axis=1))

    def normalize(u):
        c, s, p = units[u]
        res, m = st[u].pop("res"), st[u].pop("m")
        num = jnp.where(in_a, res[:BLK, :LANES], res[BLK:, :LANES])
        den = jnp.where(in_a, res[:BLK, LANES:], res[BLK:, LANES:])
        o_ref[c, s * BLK:(s + 1) * BLK, lanes_of(p)] = (num / den).astype(o_ref.dtype)
        lse = jnp.where(in_a, m[:BLK], m[BLK:]) * LN2 + jnp.log(den)
        packed[c, s] = lse if p == 0 else jnp.where(group == p, lse, packed[c, s])
        if p == N_PAIRS - 1:
            l_ref[c, s * BLK:(s + 1) * BLK, :] = packed.pop((c, s))

    stages = (scores, exponentials, weighted_values, normalize)
    for t in range(len(units) + len(stages) - 1):
        for k, stage in enumerate(stages):
            if 0 <= t - k < len(units):
                stage(t - k)


def _attn_call(bias, qkv):
    B, r, L, W = qkv.shape[0], qkv.shape[1], qkv.shape[2], qkv.shape[3] // 3
    R = min(ATTN_STEP_ROWS, L)
    CB = min(r, ATTN_STEP_ROWS // R)
    cur = lambda width, part=0: pl.BlockSpec((None, CB, R, width), lambda b, g, i: (b, g, i, part))
    prev = lambda part: pl.BlockSpec((None, CB, BLK, W),
                                     lambda b, g, i: (b, g, jnp.maximum(i * (R // BLK) - 1, 0), part))
    return pl.pallas_call(
        _attn_kernel,
        grid=(B, r // CB, L // R),
        in_specs=[pl.BlockSpec(bias.shape, lambda b, g, i: (0, 0, 0)), cur(W, 0), cur(W, 1), cur(W, 2),
                  prev(1), prev(2)],
        out_specs=[cur(W), cur(LANES)],
        out_shape=[jax.ShapeDtypeStruct((B, r, L, W), BF16), jax.ShapeDtypeStruct((B, r, L, LANES), F32)],
        compiler_params=pltpu.CompilerParams(
            dimension_semantics=("arbitrary", "arbitrary", "arbitrary"), vmem_limit_bytes=VMEM_LIMIT),
        name=f"attn_d{r}",
    )(bias, qkv, qkv, qkv, qkv, qkv)


def _out_kernel(x_ref, p_ref, o1_ref, o4_ref, o16_ref, l1_ref, l4_ref, l16_ref, sg_ref, oa_ref,
                u4_ref, u16_ref, wo_ref, gple_ref, wg_ref, bg_ref, wp_ref, out_ref):
    n_blocks = x_ref.shape[1] // PERM_ROWS
    st = [dict() for _ in range(n_blocks)]

    def rows(b):
        return slice(b * PERM_ROWS, (b + 1) * PERM_ROWS)

    def grouped(ref, b):
        r = ref.shape[0]
        n = PERM_ROWS // r
        return jnp.concatenate([ref[c, b * n:(b + 1) * n, :] for c in range(r)], axis=0)

    def to_token_order(b):
        s = st[b]
        s["o4"] = _dot(u4_ref[...], grouped(o4_ref, b))
        s["o16"] = _dot(u16_ref[...], grouped(o16_ref, b))
        for name, ref, u_ref in (("l4", l4_ref, u4_ref), ("l16", l16_ref, u16_ref)):
            cm = grouped(ref, b)
            hi = cm.astype(BF16)
            lo = (cm - hi.astype(F32)).astype(BF16)
            t = _dot(u_ref[...], jnp.concatenate([hi, lo], axis=1))
            s[name] = t[:, :LANES] + t[:, LANES:]

    def merge_weights(b):
        s = st[b]
        l1 = l1_ref[0, rows(b), :]
        m = jnp.maximum(jnp.maximum(l1, s["l4"]), s["l16"])
        e1, e4, e16 = jnp.exp(l1 - m), jnp.exp(s["l4"] - m), jnp.exp(s["l16"] - m)
        inv = 1.0 / (e1 + e4 + e16)
        s["w"] = [e1 * inv, e4 * inv]

    lane = lax.broadcasted_iota(jnp.int32, (1, LANES), 1)

    def spread(w):
        pairs = []
        for p in range(N_PAIRS):
            t = w if p == 0 else pltpu.roll(w, LANES - STAT_REP * p, 1)
            for width in (STAT_REP, 2 * STAT_REP):
                t = jnp.where(lane % (2 * width) < width, t, pltpu.roll(t, width, 1))
            pairs.append(t)
        return jnp.concatenate(pairs, axis=1)

    def expand_weights(b):
        s = st[b]
        s["wf"] = [spread(w) for w in s["w"]]

    def merge(b):
        s = st[b]
        o1 = o1_ref[0, rows(b), :].astype(F32)
        ob = s["o16"] + s["wf"][0] * (o1 - s["o16"]) + s["wf"][1] * (s["o4"] - s["o16"])
        s["mb"] = (ob * sg_ref[0, rows(b), :].astype(F32)).astype(BF16)

    def project(b):
        s = st[b]
        s["t"] = _dot(s["mb"], wo_ref[:ATTN_WIDTH, :]) + _dot(oa_ref[0, rows(b), :], wo_ref[ATTN_WIDTH:, :])

    def residual_norm(b):
        s = st[b]
        h = x_ref[0, rows(b), :] + s["t"]
        s["h"] = h
        s["n"] = (h * lax.rsqrt(jnp.mean(h * h, axis=-1, keepdims=True) + EPS) * gple_ref[...]).astype(BF16)

    def embed_dots(b):
        s = st[b]
        s["g"] = _dot(s["n"], wg_ref[...])
        s["pp"] = _dot(p_ref[0, rows(b), :].astype(BF16), wp_ref[...])

    def embed_out(b):
        s = st[b]
        out_ref[0, rows(b), :] = s["h"] + jax.nn.sigmoid(s["g"] + bg_ref[...]) * s["pp"]

    stages = (to_token_order, merge_weights, expand_weights, merge, project, residual_norm, embed_dots, embed_out)
    on_matrix_unit = (to_token_order, project, embed_dots)
    for t in range(len(stages) + n_blocks - 1):
        ready = [(stages[t - b], b) for b in range(n_blocks) if 0 <= t - b < len(stages)]
        for stage, b in sorted(ready, key=lambda sb: sb[0] not in on_matrix_unit):
            stage(b)


def _out_call(x, p, os_, ls_, sg, oa, perms, w_out, g_ple, w_gate, b_gate, w_proj):
    B, S, D = x.shape
    T = OUT_ROWS
    tok = lambda width: pl.BlockSpec((1, T, width), lambda b, i: (b, i, 0))
    cls = lambda r, width: pl.BlockSpec((None, r, T // r, width), lambda b, i: (b, 0, i, 0))
    const = lambda shape: pl.BlockSpec(shape, lambda b, i: (0,) * len(shape))
    r4, r16 = STRIDES
    W = ATTN_WIDTH
    return pl.pallas_call(
        _out_kernel,
        grid=(B, S // T),
        in_specs=[tok(D), tok(PLE_DIM), tok(W), cls(r4, W), cls(r16, W), tok(LANES), cls(r4, LANES),
                  cls(r16, LANES), tok(W), tok(GMLP_WIDTH),
                  const((PERM_ROWS, PERM_ROWS)), const((PERM_ROWS, PERM_ROWS)),
                  const(w_out.shape), const((1, D)), const(w_gate.shape), const((1, D)), const(w_proj.shape)],
        out_specs=tok(D),
        out_shape=jax.ShapeDtypeStruct((B, S, D), x.dtype),
        compiler_params=pltpu.CompilerParams(
            dimension_semantics=("arbitrary", "arbitrary"), vmem_limit_bytes=VMEM_LIMIT),
        name="out",
    )(x, p, *os_, *ls_, sg, oa, perms[r4][1], perms[r16][1], w_out, g_ple, w_gate, b_gate, w_proj)


def _band_bias():
    qi = np.arange(BLK)[:, None]
    col = np.arange(2 * BLK)[None, :]
    dist = qi + BLK - col
    band = (dist >= 0) & (dist <= BLK)
    both = np.stack([band, band & (col >= BLK)])
    bias = np.where(both, 0.0, NEG).astype(np.float32)
    return jnp.asarray(np.concatenate([bias, bias], axis=1))


def _head_average():
    h = np.arange(ATTN_WIDTH) // HEAD_DIM
    return jnp.asarray((h[:, None] == h[None, :]).astype(np.float32) / HEAD_DIM, dtype=BF16)


def _class_perms(r):
    t = np.arange(PERM_ROWS)
    grouped_pos = (t % r) * (PERM_ROWS // r) + t // r
    d = np.zeros((PERM_ROWS, PERM_ROWS), np.float32)
    d[grouped_pos, t] = 1.0
    return jnp.asarray(d, dtype=BF16), jnp.asarray(d.T, dtype=BF16)


def _inv_freq_col():
    half = HEAD_DIM // 2
    return jnp.exp(-math.log(ROPE_THETA) * jnp.arange(half, dtype=F32) / half)[:, None]


def kernel(x, p, positions, g_in, w_in, q_norm, k_norm, w_spatial, b_spatial, ln_v_g, ln_v_b, w_out,
           g_ple, w_ple_gate, b_ple_gate, w_ple_proj):
    B, S, _ = x.shape
    depth = w_in.shape[0]
    bias = _band_bias()
    avg = _head_average()
    inv_freq = _inv_freq_col()
    perms = {r: _class_perms(r) for r in STRIDES}
    pos_rows = positions.astype(F32).reshape(B, S // CHUNK, CHUNK)
    h = x
    for i in range(depth):
        qkv, qkv4, qkv16, sg, oa = _proj_call(
            h, pos_rows, inv_freq, g_in[i][None, :], w_in[i].astype(BF16), avg,
            jnp.tile(q_norm[i], N_HEADS)[None, :], jnp.tile(k_norm[i], N_HEADS)[None, :],
            w_spatial[i], jnp.repeat(b_spatial[i].T, LANES, axis=1),
            ln_v_g[i][None, :], ln_v_b[i][None, :], perms)
        o1, l1 = _attn_call(bias, qkv[:, None])
        o4, l4 = _attn_call(bias, qkv4)
        o16, l16 = _attn_call(bias, qkv16)
        h = _out_call(h, p[i], (o1[:, 0], o4, o16), (l1[:, 0], l4, l16), sg, oa, perms,
                      w_out[i].astype(BF16), g_ple[i][None, :], w_ple_gate[i].astype(BF16),
                      b_ple_gate[i][None, :], w_ple_proj[i].astype(BF16))
    return h
```

```python
import functools
import math

import jax
import jax.numpy as jnp
import numpy as np
from jax import lax
from jax.experimental import pallas as pl
from jax.experimental.pallas import tpu as pltpu

D_MODEL = 1024
N_HEADS = 8
HEAD_DIM = 64
ATTN_WIDTH = N_HEADS * HEAD_DIM
DILATIONS = ((128, 1), (512, 4), (2048, 16))
BLK = 128
ROPE_THETA = 10000.0
GMLP_GROUPS = 4
GMLP_WIDTH = D_MODEL // 2
CHUNK = 128
PLE_DIM = 256
EPS = 1e-6
NEG = -1e30
LN2 = math.log(2.0)
LOG2E = math.log2(math.e)

LANES = 128
MXU_DEPTH = 256
N_PAIRS = ATTN_WIDTH // LANES
STAT_REP = LANES // N_HEADS
VMEM_LIMIT = 60 * 1024 * 1024

PROJ_ROWS = 512
ATTN_STEP_ROWS = 2048
OUT_ROWS = 1024
PERM_ROWS = MXU_DEPTH
STRIDES = tuple(d for _, d in DILATIONS if d > 1)

BF16 = jnp.bfloat16
F32 = jnp.float32


def _dot(a, b):
    return jnp.dot(a, b, preferred_element_type=F32)


def _dot_nt(a, b):
    return lax.dot_general(a, b, (((1,), (1,)), ((), ())), preferred_element_type=F32)


def _silu(x):
    return x * jax.nn.sigmoid(x)


def _project_pieces(x_ref, gin_ref, w_ref, z_ref):
    n_rows = x_ref.shape[1]
    half = n_rows // 2
    normed = {}

    def piece(h, j):
        rows = slice(h * half, (h + 1) * half)
        if h not in normed:
            x = x_ref[0, rows, :]
            scale = lax.rsqrt(jnp.mean(x * x, axis=-1, keepdims=True) + EPS)
            normed[h] = (x * scale * gin_ref[...]).astype(BF16)
        cols = slice(j * ATTN_WIDTH, (j + 1) * ATTN_WIDTH)
        z_ref[rows, cols] = _dot(normed[h], w_ref[:, cols])

    return [(lambda h=h, j=j: piece(h, j)) for h in range(2) for j in range(w_ref.shape[1] // ATTN_WIDTH)]


def _finish_pieces(first_block, z_ref, pos_ref, invf_ref, avg_ref, qg_ref, kg_ref, ws_ref, bs_ref, lng_ref, lnb_ref,
                   d4_ref, d16_ref, qkv_ref, qkv4_ref, qkv16_ref, sg_ref, oa_ref):
    n_rows = z_ref.shape[0]
    blocks = [slice(b * CHUNK, (b + 1) * CHUNK) for b in range(n_rows // CHUNK)]
    lane = lax.broadcasted_iota(jnp.int32, (1, LANES), 1)
    first_half = (lane % HEAD_DIM) < (HEAD_DIM // 2)
    cos, sin_signed, vn = {}, {}, {}

    def cols(j):
        return slice(j * ATTN_WIDTH, (j + 1) * ATTN_WIDTH)

    def tables(b):
        pos_row = pos_ref[0, pl.ds(first_block + b, 1), :]
        ang = pos_row * invf_ref[...]
        to_rows = lambda t: jnp.concatenate([t] * (LANES // t.shape[0]), axis=0).T
        cos[b] = to_rows(jnp.cos(ang))
        sin = to_rows(jnp.sin(ang))
        sin_signed[b] = jnp.where(first_half, -sin, sin)

    def norm_rope(b):
        zq, zk = z_ref[blocks[b], cols(0)], z_ref[blocks[b], cols(1)]
        sq = jnp.concatenate([(zq * zq).astype(BF16), (zk * zk).astype(BF16)], axis=0)
        ms = jnp.concatenate([_dot(sq[:, c:c + MXU_DEPTH], avg_ref[c:c + MXU_DEPTH, c:c + MXU_DEPTH])
                              for c in range(0, ATTN_WIDTH, MXU_DEPTH)], axis=1)
        for j, (z, g_ref, scale) in enumerate(((zq, qg_ref, HEAD_DIM ** -0.5 * LOG2E), (zk, kg_ref, 1.0))):
            y = z * lax.rsqrt(ms[j * CHUNK:(j + 1) * CHUNK] + EPS) * g_ref[...]
            outs = []
            for p in range(N_PAIRS):
                yp = y[:, p * LANES:(p + 1) * LANES]
                swapped = jnp.where(first_half, pltpu.roll(yp, LANES - HEAD_DIM // 2, 1),
                                    pltpu.roll(yp, HEAD_DIM // 2, 1))
                outs.append(yp * cos[b] + swapped * sin_signed[b])
            r = jnp.concatenate(outs, axis=1)
            if scale != 1.0:
                r = r * scale
            qkv_ref[0, blocks[b], cols(j)] = r.astype(BF16)

    def value(b):
        qkv_ref[0, blocks[b], cols(2)] = z_ref[blocks[b], cols(2)].astype(BF16)

    def regroup(j, part):
        piece = qkv_ref[0, j * PERM_ROWS:(j + 1) * PERM_ROWS, cols(part)]
        for (d_ref, out_ref) in ((d4_ref, qkv4_ref), (d16_ref, qkv16_ref)):
            r = out_ref.shape[1]
            n = PERM_ROWS // r
            cm = _dot(d_ref[...], piece).astype(BF16)
            for c in range(r):
                out_ref[0, c, j * n:(j + 1) * n, cols(part)] = cm[c * n:(c + 1) * n]

    def gate(b):
        sg_ref[0, blocks[b], :] = _silu(z_ref[blocks[b], cols(3)]).astype(BF16)

    def gmlp_pre(b):
        gv = jax.nn.gelu(z_ref[blocks[b], cols(5)])
        mu = jnp.mean(gv, axis=-1, keepdims=True)
        gc = gv - mu
        var = jnp.mean(gc * gc, axis=-1, keepdims=True)
        vn[b] = (gc * lax.rsqrt(var + EPS) * lng_ref[...] + lnb_ref[...]).astype(BF16)

    def gmlp_post(b):
        r_i = lax.broadcasted_iota(jnp.int32, (CHUNK, CHUNK), 0)
        c_i = lax.broadcasted_iota(jnp.int32, (CHUNK, CHUNK), 1)
        sv = jnp.concatenate(
            [_dot(jnp.where(c_i <= r_i, ws_ref[g], 0.0).astype(BF16), vn[b][:, g * LANES:(g + 1) * LANES])
             for g in range(GMLP_GROUPS)], axis=1) + bs_ref[...]
        u = jax.nn.gelu(z_ref[blocks[b], cols(4)])
        ga = _silu(z_ref[blocks[b], cols(6)])
        oa_ref[0, blocks[b], :] = (u * sv * ga).astype(BF16)

    nb = range(len(blocks))
    nc = range(n_rows // PERM_ROWS)
    pieces = []
    for b in nb:
        pieces += [(2.5, lambda b=b: tables(b)),
                   (6.0, lambda b=b: norm_rope(b)),
                   (0.5, lambda b=b: value(b))]
    pieces += [(1.0, lambda b=b: gate(b)) for b in nb]
    pieces += [(2.0, lambda j=j: regroup(j, 0)) for j in nc]
    pieces += [(3.5, lambda b=b: gmlp_pre(b)) for b in nb]
    pieces += [(2.0, lambda j=j: regroup(j, 1)) for j in nc]
    pieces += [(2.0, lambda j=j: regroup(j, 2)) for j in nc]
    pieces += [(4.0, lambda b=b: gmlp_post(b)) for b in nb]
    return pieces


def _interleave(weighted, uniform):
    total = sum(w for w, _ in weighted)
    out, taken, done = [], 0, 0.0
    for w, piece in weighted:
        while taken < len(uniform) and done >= (taken + 0.5) * total / len(uniform):
            out.append(uniform[taken])
            taken += 1
        out.append(piece)
        done += w
    return out + uniform[taken:]


def _proj_kernel(tiles_per_seq, x_ref, pos_ref, invf_ref, gin_ref, w_ref, *rest):
    finish_refs, (z0_ref, z1_ref, wb_ref) = rest[:-3], rest[-3:]
    step = pl.program_id(0)
    first_block = (jnp.maximum(step - 1, 0) % tiles_per_seq) * (x_ref.shape[1] // CHUNK)

    @pl.when(step == 0)
    def _():
        z1_ref[...] = jnp.zeros_like(z1_ref)
        for c in range(0, w_ref.shape[1], ATTN_WIDTH):
            wb_ref[:, c:c + ATTN_WIDTH] = w_ref[:, c:c + ATTN_WIDTH].astype(BF16)

    def run(z_write_ref, z_read_ref):
        for piece in _interleave(_finish_pieces(first_block, z_read_ref, pos_ref, invf_ref, *finish_refs),
                                 _project_pieces(x_ref, gin_ref, wb_ref, z_write_ref)):
            piece()

    @pl.when(step % 2 == 0)
    def _():
        run(z0_ref, z1_ref)

    @pl.when(step % 2 == 1)
    def _():
        run(z1_ref, z0_ref)


def _proj_call(x, pos_rows, inv_freq, g_in, w_in, avg, qg, kg, w_s, b_tab, ln_g, ln_b, perms):
    B, S, D = x.shape
    T = PROJ_ROWS
    n_cols = w_in.shape[1]
    W = ATTN_WIDTH
    per_seq = S // T
    n_tiles = B * per_seq
    proj_tile = lambda s: jnp.minimum(s, n_tiles - 1)
    done_tile = lambda s: jnp.maximum(s - 1, 0)
    tok_in = pl.BlockSpec((1, T, D), lambda s: (proj_tile(s) // per_seq, proj_tile(s) % per_seq, 0))
    tok = lambda width: pl.BlockSpec((1, T, width), lambda s: (done_tile(s) // per_seq, done_tile(s) % per_seq, 0))
    cls = lambda r: pl.BlockSpec((1, r, T // r, 3 * W),
                                 lambda s: (done_tile(s) // per_seq, 0, done_tile(s) % per_seq, 0))
    seq_pos = pl.BlockSpec((1, S // CHUNK, CHUNK), lambda s: (done_tile(s) // per_seq, 0, 0))
    const = lambda shape: pl.BlockSpec(shape, lambda s: (0,) * len(shape))
    nat_sds = lambda width: jax.ShapeDtypeStruct((B, S, width), BF16)
    cls_sds = lambda r: jax.ShapeDtypeStruct((B, r, S // r, 3 * W), BF16)
    r4, r16 = STRIDES
    return pl.pallas_call(
        functools.partial(_proj_kernel, per_seq),
        grid=(n_tiles + 1,),
        scratch_shapes=[pltpu.VMEM((T, n_cols), F32), pltpu.VMEM((T, n_cols), F32), pltpu.VMEM((D, n_cols), BF16)],
        in_specs=[tok_in, seq_pos, const(inv_freq.shape), const((1, D)), const((D, n_cols)),
                  const((W, W)), const((1, W)), const((1, W)),
                  const((GMLP_GROUPS, CHUNK, CHUNK)), const((CHUNK, GMLP_WIDTH)),
                  const((1, GMLP_WIDTH)), const((1, GMLP_WIDTH)),
                  const((PERM_ROWS, PERM_ROWS)), const((PERM_ROWS, PERM_ROWS))],
        out_specs=[tok(3 * W), cls(r4), cls(r16), tok(W), tok(W)],
        out_shape=[nat_sds(3 * W), cls_sds(r4), cls_sds(r16), nat_sds(W), nat_sds(W)],
        compiler_params=pltpu.CompilerParams(dimension_semantics=("arbitrary",), vmem_limit_bytes=VMEM_LIMIT),
        name="proj",
    )(x, pos_rows, inv_freq, g_in, w_in, avg, qg, kg, w_s, b_tab, ln_g, ln_b, perms[r4][0], perms[r16][0])


def _attn_kernel(bias_ref, q_ref, k_ref, v_ref, kp_ref, vp_ref, o_ref, l_ref):
    n_cls, n_rows = q_ref.shape[0], q_ref.shape[1]
    first_tile = pl.program_id(2) == 0
    lane = lax.broadcasted_iota(jnp.int32, (1, LANES), 1)
    in_a = lane < HEAD_DIM
    group = lane // STAT_REP % N_PAIRS
    ones = jnp.ones((2 * BLK, LANES), BF16)
    blocks_per_cls = n_rows // BLK
    units = [(c, s, p) for c in range(n_cls) for s in range(blocks_per_cls) for p in range(N_PAIRS)]
    st = [dict() for _ in units]
    packed = {}

    def lanes_of(p):
        return slice(p * LANES, (p + 1) * LANES)

    def window(cur_ref, prev_ref, c, s, p):
        if s == 0:
            return jnp.concatenate([prev_ref[c, :, lanes_of(p)], cur_ref[c, :BLK, lanes_of(p)]], axis=0)
        return cur_ref[c, (s - 1) * BLK:(s + 1) * BLK, lanes_of(p)]

    def scores(u):
        c, s, p = units[u]
        qp = q_ref[c, s * BLK:(s + 1) * BLK, lanes_of(p)]
        zero = jnp.zeros_like(qp)
        q2 = jnp.concatenate([jnp.where(in_a, qp, zero), jnp.where(in_a, zero, qp)], axis=0)
        st[u]["sc"] = _dot_nt(q2, window(k_ref, kp_ref, c, s, p))

    def exponentials(u):
        _, s, _ = units[u]
        bias = bias_ref[first_tile.astype(jnp.int32)] if s == 0 else bias_ref[0]
        sc = st[u].pop("sc") + bias
        m = jnp.max(sc, axis=-1, keepdims=True)
        st[u]["m"] = m
        st[u]["e"] = jnp.exp2(sc - m).astype(BF16)

    def weighted_values(u):
        c, s, p = units[u]
        vv = window(v_ref, vp_ref, c, s, p)
        st[u]["res"] = _dot(st[u].pop("e"), jnp.concatenate([vv, ones], axis=1))

    def normalize(u):
        c, s, p = units[u]
        res, m = st[u].pop("res"), st[u].pop("m")
        num = jnp.where(in_a, res[:BLK, :LANES], res[BLK:, :LANES])
        den = jnp.where(in_a, res[:BLK, LANES:], res[BLK:, LANES:])
        o_ref[c, s * BLK:(s + 1) * BLK, lanes_of(p)] = (num / den).astype(o_ref.dtype)
        lse = jnp.where(in_a, m[:BLK], m[BLK:]) * LN2 + jnp.log(den)
        packed[c, s] = lse if p == 0 else jnp.where(group == p, lse, packed[c, s])
        if p == N_PAIRS - 1:
            l_ref[c, s * BLK:(s + 1) * BLK, :] = packed.pop((c, s))

    stages = (scores, exponentials, weighted_values, normalize)
    for t in range(len(units) + len(stages) - 1):
        for k, stage in enumerate(stages):
            if 0 <= t - k < len(units):
                stage(t - k)


def _attn_call(bias, qkv):
    B, r, L, W = qkv.shape[0], qkv.shape[1], qkv.shape[2], qkv.shape[3] // 3
    R = min(ATTN_STEP_ROWS, L)
    CB = min(r, ATTN_STEP_ROWS // R)
    cur = lambda width, part=0: pl.BlockSpec((None, CB, R, width), lambda b, g, i: (b, g, i, part))
    prev = lambda part: pl.BlockSpec((None, CB, BLK, W),
                                     lambda b, g, i: (b, g, jnp.maximum(i * (R // BLK) - 1, 0), part))
    return pl.pallas_call(
        _attn_kernel,
        grid=(B, r // CB, L // R),
        in_specs=[pl.BlockSpec(bias.shape, lambda b, g, i: (0, 0, 0)), cur(W, 0), cur(W, 1), cur(W, 2),
                  prev(1), prev(2)],
        out_specs=[cur(W), cur(LANES)],
        out_shape=[jax.ShapeDtypeStruct((B, r, L, W), BF16), jax.ShapeDtypeStruct((B, r, L, LANES), F32)],
        compiler_params=pltpu.CompilerParams(
            dimension_semantics=("arbitrary", "arbitrary", "arbitrary"), vmem_limit_bytes=VMEM_LIMIT),
        name=f"attn_d{r}",
    )(bias, qkv, qkv, qkv, qkv, qkv)


def _out_kernel(x_ref, p_ref, o1_ref, o4_ref, o16_ref, l1_ref, l4_ref, l16_ref, sg_ref, oa_ref,
                u4_ref, u16_ref, wo32_ref, gple_ref, wg32_ref, bg_ref, wp32_ref, out_ref, wo_ref, wg_ref, wp_ref):
    n_blocks = x_ref.shape[1] // PERM_ROWS

    @pl.when(jnp.logical_and(pl.program_id(0) == 0, pl.program_id(1) == 0))
    def _():
        for src, dst in ((wo32_ref, wo_ref), (wg32_ref, wg_ref), (wp32_ref, wp_ref)):
            for r in range(0, src.shape[0], MXU_DEPTH):
                dst[r:r + MXU_DEPTH, :] = src[r:r + MXU_DEPTH, :].astype(BF16)
    st = [dict() for _ in range(n_blocks)]

    def rows(b):
        return slice(b * PERM_ROWS, (b + 1) * PERM_ROWS)

    def grouped(ref, b):
        r = ref.shape[0]
        n = PERM_ROWS // r
        return jnp.concatenate([ref[c, b * n:(b + 1) * n, :] for c in range(r)], axis=0)

    def to_token_order(b):
        s = st[b]
        s["o4"] = _dot(u4_ref[...], grouped(o4_ref, b))
        s["o16"] = _dot(u16_ref[...], grouped(o16_ref, b))
        for name, ref, u_ref in (("l4", l4_ref, u4_ref), ("l16", l16_ref, u16_ref)):
            cm = grouped(ref, b)
            hi = cm.astype(BF16)
            lo = (cm - hi.astype(F32)).astype(BF16)
            t = _dot(u_ref[...], jnp.concatenate([hi, lo], axis=1))
            s[name] = t[:, :LANES] + t[:, LANES:]

    def merge_weights(b):
        s = st[b]
        l1 = l1_ref[0, rows(b), :]
        m = jnp.maximum(jnp.maximum(l1, s["l4"]), s["l16"])
        e1, e4, e16 = jnp.exp(l1 - m), jnp.exp(s["l4"] - m), jnp.exp(s["l16"] - m)
        inv = 1.0 / (e1 + e4 + e16)
        s["w"] = [e1 * inv, e4 * inv]

    lane = lax.broadcasted_iota(jnp.int32, (1, LANES), 1)

    def spread(w):
        pairs = []
        for p in range(N_PAIRS):
            t = w if p == 0 else pltpu.roll(w, LANES - STAT_REP * p, 1)
            for width in (STAT_REP, 2 * STAT_REP):
                t = jnp.where(lane % (2 * width) < width, t, pltpu.roll(t, width, 1))
            pairs.append(t)
        return jnp.concatenate(pairs, axis=1)

    def expand_weights(b):
        s = st[b]
        s["wf"] = [spread(w) for w in s["w"]]

    def merge(b):
        s = st[b]
        o1 = o1_ref[0, rows(b), :].astype(F32)
        ob = s["o16"] + s["wf"][0] * (o1 - s["o16"]) + s["wf"][1] * (s["o4"] - s["o16"])
        s["mb"] = (ob * sg_ref[0, rows(b), :].astype(F32)).astype(BF16)

    def project(b):
        s = st[b]
        s["t"] = _dot(s["mb"], wo_ref[:ATTN_WIDTH, :]) + _dot(oa_ref[0, rows(b), :], wo_ref[ATTN_WIDTH:, :])

    def residual_norm(b):
        s = st[b]
        h = x_ref[0, rows(b), :] + s["t"]
        s["h"] = h
        s["n"] = (h * lax.rsqrt(jnp.mean(h * h, axis=-1, keepdims=True) + EPS) * gple_ref[...]).astype(BF16)

    def embed_dots(b):
        s = st[b]
        s["g"] = _dot(s["n"], wg_ref[...])
        s["pp"] = _dot(p_ref[0, rows(b), :].astype(BF16), wp_ref[...])

    def embed_out(b):
        s = st[b]
        out_ref[0, rows(b), :] = s["h"] + jax.nn.sigmoid(s["g"] + bg_ref[...]) * s["pp"]

    stages = (to_token_order, merge_weights, expand_weights, merge, project, residual_norm, embed_dots, embed_out)
    on_matrix_unit = (to_token_order, project, embed_dots)
    for t in range(len(stages) + n_blocks - 1):
        ready = [(stages[t - b], b) for b in range(n_blocks) if 0 <= t - b < len(stages)]
        for stage, b in sorted(ready, key=lambda sb: sb[0] not in on_matrix_unit):
            stage(b)


def _out_call(x, p, os_, ls_, sg, oa, perms, w_out, g_ple, w_gate, b_gate, w_proj):
    B, S, D = x.shape
    T = OUT_ROWS
    tok = lambda width: pl.BlockSpec((1, T, width), lambda b, i: (b, i, 0))
    cls = lambda r, width: pl.BlockSpec((None, r, T // r, width), lambda b, i: (b, 0, i, 0))
    const = lambda shape: pl.BlockSpec(shape, lambda b, i: (0,) * len(shape))
    r4, r16 = STRIDES
    W = ATTN_WIDTH
    return pl.pallas_call(
        _out_kernel,
        grid=(B, S // T),
        in_specs=[tok(D), tok(PLE_DIM), tok(W), cls(r4, W), cls(r16, W), tok(LANES), cls(r4, LANES),
                  cls(r16, LANES), tok(W), tok(GMLP_WIDTH),
                  const((PERM_ROWS, PERM_ROWS)), const((PERM_ROWS, PERM_ROWS)),
                  const(w_out.shape), const((1, D)), const(w_gate.shape), const((1, D)), const(w_proj.shape)],
        out_specs=tok(D),
        out_shape=jax.ShapeDtypeStruct((B, S, D), x.dtype),
        scratch_shapes=[pltpu.VMEM(w.shape, BF16) for w in (w_out, w_gate, w_proj)],
        compiler_params=pltpu.CompilerParams(
            dimension_semantics=("arbitrary", "arbitrary"), vmem_limit_bytes=VMEM_LIMIT),
        name="out",
    )(x, p, *os_, *ls_, sg, oa, perms[r4][1], perms[r16][1], w_out, g_ple, w_gate, b_gate, w_proj)


def _band_bias():
    qi = np.arange(BLK)[:, None]
    col = np.arange(2 * BLK)[None, :]
    dist = qi + BLK - col
    band = (dist >= 0) & (dist <= BLK)
    both = np.stack([band, band & (col >= BLK)])
    bias = np.where(both, 0.0, NEG).astype(np.float32)
    return jnp.asarray(np.concatenate([bias, bias], axis=1))


def _head_average():
    h = np.arange(ATTN_WIDTH) // HEAD_DIM
    return jnp.asarray((h[:, None] == h[None, :]).astype(np.float32) / HEAD_DIM, dtype=BF16)


def _class_perms(r):
    t = np.arange(PERM_ROWS)
    grouped_pos = (t % r) * (PERM_ROWS // r) + t // r
    d = np.zeros((PERM_ROWS, PERM_ROWS), np.float32)
    d[grouped_pos, t] = 1.0
    return jnp.asarray(d, dtype=BF16), jnp.asarray(d.T, dtype=BF16)


def _inv_freq_col():
    half = HEAD_DIM // 2
    return jnp.exp(-math.log(ROPE_THETA) * jnp.arange(half, dtype=F32) / half)[:, None]


def kernel(x, p, positions, g_in, w_in, q_norm, k_norm, w_spatial, b_spatial, ln_v_g, ln_v_b, w_out,
           g_ple, w_ple_gate, b_ple_gate, w_ple_proj):
    B, S, _ = x.shape
    depth = w_in.shape[0]
    bias = _band_bias()
    avg = _head_average()
    inv_freq = _inv_freq_col()
    perms = {r: _class_perms(r) for r in STRIDES}
    pos_rows = positions.astype(F32).reshape(B, S // CHUNK, CHUNK)
    h = x
    for i in range(depth):
        qkv, qkv4, qkv16, sg, oa = _proj_call(
            h, pos_rows, inv_freq, g_in[i][None, :], w_in[i], avg,
            jnp.tile(q_norm[i], N_HEADS)[None, :], jnp.tile(k_norm[i], N_HEADS)[None, :],
            w_spatial[i], jnp.repeat(b_spatial[i].T, LANES, axis=1),
            ln_v_g[i][None, :], ln_v_b[i][None, :], perms)
        o1, l1 = _attn_call(bias, qkv[:, None])
        o4, l4 = _attn_call(bias, qkv4)
        o16, l16 = _attn_call(bias, qkv16)
        h = _out_call(h, p[i], (o1[:, 0], o4, o16), (l1[:, 0], l4, l16), sg, oa, perms,
                      w_out[i], g_ple[i][None, :], w_ple_gate[i], b_ple_gate[i][None, :], w_ple_proj[i])
    return h
```

```python
import functools
import math

import jax
import jax.numpy as jnp
import numpy as np
from jax import lax
from jax.experimental import pallas as pl
from jax.experimental.pallas import tpu as pltpu

D_MODEL = 1024
N_HEADS = 8
HEAD_DIM = 64
ATTN_WIDTH = N_HEADS * HEAD_DIM
DILATIONS = ((128, 1), (512, 4), (2048, 16))
BLK = 128
ROPE_THETA = 10000.0
GMLP_GROUPS = 4
GMLP_WIDTH = D_MODEL // 2
CHUNK = 128
PLE_DIM = 256
EPS = 1e-6
NEG = -1e30
LN2 = math.log(2.0)
LOG2E = math.log2(math.e)

LANES = 128
MXU_DEPTH = 256
N_PAIRS = ATTN_WIDTH // LANES
STAT_REP = LANES // N_HEADS
VMEM_LIMIT = 60 * 1024 * 1024

PROJ_ROWS = 512
ATTN_STEP_ROWS = 4096
OUT_ROWS = 1024
PERM_ROWS = MXU_DEPTH
STRIDES = tuple(d for _, d in DILATIONS if d > 1)

BF16 = jnp.bfloat16
F32 = jnp.float32


def _dot(a, b):
    return jnp.dot(a, b, preferred_element_type=F32)


def _dot_nt(a, b):
    return lax.dot_general(a, b, (((1,), (1,)), ((), ())), preferred_element_type=F32)


def _silu(x):
    return x * jax.nn.sigmoid(x)


def _project_pieces(x_ref, gin_ref, w_ref, z_ref):
    n_rows = x_ref.shape[1]
    half = n_rows // 2
    normed = {}

    def piece(h, j):
        rows = slice(h * half, (h + 1) * half)
        if h not in normed:
            x = x_ref[0, rows, :]
            scale = lax.rsqrt(jnp.mean(x * x, axis=-1, keepdims=True) + EPS)
            normed[h] = (x * scale * gin_ref[...]).astype(BF16)
        cols = slice(j * ATTN_WIDTH, (j + 1) * ATTN_WIDTH)
        z_ref[rows, cols] = _dot(normed[h], w_ref[:, cols])

    return [(lambda h=h, j=j: piece(h, j)) for h in range(2) for j in range(w_ref.shape[1] // ATTN_WIDTH)]


def _finish_pieces(first_block, z_ref, pos_ref, invf_ref, avg_ref, qg_ref, kg_ref, ws_ref, bs_ref, lng_ref, lnb_ref,
                   d4_ref, d16_ref, qkv_ref, qkv4_ref, qkv16_ref, sg_ref, oa_ref):
    n_rows = z_ref.shape[0]
    blocks = [slice(b * CHUNK, (b + 1) * CHUNK) for b in range(n_rows // CHUNK)]
    lane = lax.broadcasted_iota(jnp.int32, (1, LANES), 1)
    first_half = (lane % HEAD_DIM) < (HEAD_DIM // 2)
    cos, sin_signed, vn = {}, {}, {}

    def cols(j):
        return slice(j * ATTN_WIDTH, (j + 1) * ATTN_WIDTH)

    def tables(b):
        pos_row = pos_ref[0, pl.ds(first_block + b, 1), :]
        ang = pos_row * invf_ref[...]
        to_rows = lambda t: jnp.concatenate([t] * (LANES // t.shape[0]), axis=0).T
        cos[b] = to_rows(jnp.cos(ang))
        sin = to_rows(jnp.sin(ang))
        sin_signed[b] = jnp.where(first_half, -sin, sin)

    def norm_rope(b):
        zq, zk = z_ref[blocks[b], cols(0)], z_ref[blocks[b], cols(1)]
        sq = jnp.concatenate([(zq * zq).astype(BF16), (zk * zk).astype(BF16)], axis=0)
        ms = jnp.concatenate([_dot(sq[:, c:c + MXU_DEPTH], avg_ref[c:c + MXU_DEPTH, c:c + MXU_DEPTH])
                              for c in range(0, ATTN_WIDTH, MXU_DEPTH)], axis=1)
        for j, (z, g_ref, scale) in enumerate(((zq, qg_ref, HEAD_DIM ** -0.5 * LOG2E), (zk, kg_ref, 1.0))):
            y = z * lax.rsqrt(ms[j * CHUNK:(j + 1) * CHUNK] + EPS) * g_ref[...]
            outs = []
            for p in range(N_PAIRS):
                yp = y[:, p * LANES:(p + 1) * LANES]
                swapped = jnp.where(first_half, pltpu.roll(yp, LANES - HEAD_DIM // 2, 1),
                                    pltpu.roll(yp, HEAD_DIM // 2, 1))
                outs.append(yp * cos[b] + swapped * sin_signed[b])
            r = jnp.concatenate(outs, axis=1)
            if scale != 1.0:
                r = r * scale
            qkv_ref[0, blocks[b], cols(j)] = r.astype(BF16)

    def value(b):
        qkv_ref[0, blocks[b], cols(2)] = z_ref[blocks[b], cols(2)].astype(BF16)

    def regroup(j, part):
        piece = qkv_ref[0, j * PERM_ROWS:(j + 1) * PERM_ROWS, cols(part)]
        for (d_ref, out_ref) in ((d4_ref, qkv4_ref), (d16_ref, qkv16_ref)):
            r = out_ref.shape[1]
            n = PERM_ROWS // r
            cm = _dot(d_ref[...], piece).astype(BF16)
            for c in range(r):
                out_ref[0, c, j * n:(j + 1) * n, cols(part)] = cm[c * n:(c + 1) * n]

    def gate(b):
        sg_ref[0, blocks[b], :] = _silu(z_ref[blocks[b], cols(3)]).astype(BF16)

    def gmlp_pre(b):
        gv = jax.nn.gelu(z_ref[blocks[b], cols(5)])
        mu = jnp.mean(gv, axis=-1, keepdims=True)
        gc = gv - mu
        var = jnp.mean(gc * gc, axis=-1, keepdims=True)
        vn[b] = (gc * lax.rsqrt(var + EPS) * lng_ref[...] + lnb_ref[...]).astype(BF16)

    def gmlp_post(b):
        r_i = lax.broadcasted_iota(jnp.int32, (CHUNK, CHUNK), 0)
        c_i = lax.broadcasted_iota(jnp.int32, (CHUNK, CHUNK), 1)
        sv = jnp.concatenate(
            [_dot(jnp.where(c_i <= r_i, ws_ref[g], 0.0).astype(BF16), vn[b][:, g * LANES:(g + 1) * LANES])
             for g in range(GMLP_GROUPS)], axis=1) + bs_ref[...]
        u = jax.nn.gelu(z_ref[blocks[b], cols(4)])
        ga = _silu(z_ref[blocks[b], cols(6)])
        oa_ref[0, blocks[b], :] = (u * sv * ga).astype(BF16)

    nb = range(len(blocks))
    nc = range(n_rows // PERM_ROWS)
    pieces = []
    for b in nb:
        pieces += [(2.5, lambda b=b: tables(b)),
                   (6.0, lambda b=b: norm_rope(b)),
                   (0.5, lambda b=b: value(b))]
    pieces += [(1.0, lambda b=b: gate(b)) for b in nb]
    pieces += [(2.0, lambda j=j: regroup(j, 0)) for j in nc]
    pieces += [(3.5, lambda b=b: gmlp_pre(b)) for b in nb]
    pieces += [(2.0, lambda j=j: regroup(j, 1)) for j in nc]
    pieces += [(2.0, lambda j=j: regroup(j, 2)) for j in nc]
    pieces += [(4.0, lambda b=b: gmlp_post(b)) for b in nb]
    return pieces


def _interleave(weighted, uniform):
    total = sum(w for w, _ in weighted)
    out, taken, done = [], 0, 0.0
    for w, piece in weighted:
        while taken < len(uniform) and done >= (taken + 0.5) * total / len(uniform):
            out.append(uniform[taken])
            taken += 1
        out.append(piece)
        done += w
    return out + uniform[taken:]


def _proj_kernel(tiles_per_seq, x_ref, pos_ref, invf_ref, gin_ref, w_ref, *rest):
    finish_refs, (z0_ref, z1_ref, wb_ref) = rest[:-3], rest[-3:]
    step = pl.program_id(0)
    first_block = (jnp.maximum(step - 1, 0) % tiles_per_seq) * (x_ref.shape[1] // CHUNK)

    @pl.when(step == 0)
    def _():
        z1_ref[...] = jnp.zeros_like(z1_ref)
        for c in range(0, w_ref.shape[1], ATTN_WIDTH):
            wb_ref[:, c:c + ATTN_WIDTH] = w_ref[:, c:c + ATTN_WIDTH].astype(BF16)

    def run(z_write_ref, z_read_ref):
        for piece in _interleave(_finish_pieces(first_block, z_read_ref, pos_ref, invf_ref, *finish_refs),
                                 _project_pieces(x_ref, gin_ref, wb_ref, z_write_ref)):
            piece()

    @pl.when(step % 2 == 0)
    def _():
        run(z0_ref, z1_ref)

    @pl.when(step % 2 == 1)
    def _():
        run(z1_ref, z0_ref)


def _proj_call(x, pos_rows, inv_freq, g_in, w_in, avg, qg, kg, w_s, b_tab, ln_g, ln_b, perms):
    B, S, D = x.shape
    T = PROJ_ROWS
    n_cols = w_in.shape[1]
    W = ATTN_WIDTH
    per_seq = S // T
    n_tiles = B * per_seq
    proj_tile = lambda s: jnp.minimum(s, n_tiles - 1)
    done_tile = lambda s: jnp.maximum(s - 1, 0)
    tok_in = pl.BlockSpec((1, T, D), lambda s: (proj_tile(s) // per_seq, proj_tile(s) % per_seq, 0))
    tok = lambda width: pl.BlockSpec((1, T, width), lambda s: (done_tile(s) // per_seq, done_tile(s) % per_seq, 0))
    cls = lambda r: pl.BlockSpec((1, r, T // r, 3 * W),
                                 lambda s: (done_tile(s) // per_seq, 0, done_tile(s) % per_seq, 0))
    seq_pos = pl.BlockSpec((1, S // CHUNK, CHUNK), lambda s: (done_tile(s) // per_seq, 0, 0))
    const = lambda shape: pl.BlockSpec(shape, lambda s: (0,) * len(shape))
    nat_sds = lambda width: jax.ShapeDtypeStruct((B, S, width), BF16)
    cls_sds = lambda r: jax.ShapeDtypeStruct((B, r, S // r, 3 * W), BF16)
    r4, r16 = STRIDES
    return pl.pallas_call(
        functools.partial(_proj_kernel, per_seq),
        grid=(n_tiles + 1,),
        scratch_shapes=[pltpu.VMEM((T, n_cols), F32), pltpu.VMEM((T, n_cols), F32), pltpu.VMEM((D, n_cols), BF16)],
        in_specs=[tok_in, seq_pos, const(inv_freq.shape), const((1, D)), const((D, n_cols)),
                  const((W, W)), const((1, W)), const((1, W)),
                  const((GMLP_GROUPS, CHUNK, CHUNK)), const((CHUNK, GMLP_WIDTH)),
                  const((1, GMLP_WIDTH)), const((1, GMLP_WIDTH)),
                  const((PERM_ROWS, PERM_ROWS)), const((PERM_ROWS, PERM_ROWS))],
        out_specs=[tok(3 * W), cls(r4), cls(r16), tok(W), tok(W)],
        out_shape=[nat_sds(3 * W), cls_sds(r4), cls_sds(r16), nat_sds(W), nat_sds(W)],
        compiler_params=pltpu.CompilerParams(dimension_semantics=("arbitrary",), vmem_limit_bytes=VMEM_LIMIT),
        name="proj",
    )(x, pos_rows, inv_freq, g_in, w_in, avg, qg, kg, w_s, b_tab, ln_g, ln_b, perms[r4][0], perms[r16][0])


def _attn_kernel(bias_ref, q_ref, k_ref, v_ref, kp_ref, vp_ref, o_ref, l_ref):
    n_cls, n_rows = q_ref.shape[0], q_ref.shape[1]
    first_tile = pl.program_id(2) == 0
    lane = lax.broadcasted_iota(jnp.int32, (1, LANES), 1)
    in_a = lane < HEAD_DIM
    group = lane // STAT_REP % N_PAIRS
    ones = jnp.ones((2 * BLK, LANES), BF16)
    blocks_per_cls = n_rows // BLK
    units = [(c, s, p) for c in range(n_cls) for s in range(blocks_per_cls) for p in range(N_PAIRS)]
    st = [dict() for _ in units]
    packed = {}

    def lanes_of(p):
        return slice(p * LANES, (p + 1) * LANES)

    def window(cur_ref, prev_ref, c, s, p):
        if s == 0:
            return jnp.concatenate([prev_ref[c, :, lanes_of(p)], cur_ref[c, :BLK, lanes_of(p)]], axis=0)
        return cur_ref[c, (s - 1) * BLK:(s + 1) * BLK, lanes_of(p)]

    def scores(u):
        c, s, p = units[u]
        qp = q_ref[c, s * BLK:(s + 1) * BLK, lanes_of(p)]
        zero = jnp.zeros_like(qp)
        q2 = jnp.concatenate([jnp.where(in_a, qp, zero), jnp.where(in_a, zero, qp)], axis=0)
        st[u]["sc"] = _dot_nt(q2, window(k_ref, kp_ref, c, s, p))

    def exponentials(u):
        _, s, _ = units[u]
        bias = bias_ref[first_tile.astype(jnp.int32)] if s == 0 else bias_ref[0]
        sc = st[u].pop("sc") + bias
        m = jnp.max(sc, axis=-1, keepdims=True)
        st[u]["m"] = m
        st[u]["e"] = jnp.exp2(sc - m).astype(BF16)

    def weighted_values(u):
        c, s, p = units[u]
        vv = window(v_ref, vp_ref, c, s, p)
        st[u]["res"] = _dot(st[u].pop("e"), jnp.concatenate([vv, ones], axis=1))

    def normalize(u):
        c, s, p = units[u]
        res, m = st[u].pop("res"), st[u].pop("m")
        num = jnp.where(in_a, res[:BLK, :LANES], res[BLK:, :LANES])
        den = jnp.where(in_a, res[:BLK, LANES:], res[BLK:, LANES:])
        o_ref[c, s * BLK:(s + 1) * BLK, lanes_of(p)] = (num / den).astype(o_ref.dtype)
        lse = jnp.where(in_a, m[:BLK], m[BLK:]) * LN2 + jnp.log(den)
        packed[c, s] = lse if p == 0 else jnp.where(group == p, lse, packed[c, s])
        if p == N_PAIRS - 1:
            l_ref[c, s * BLK:(s + 1) * BLK, :] = packed.pop((c, s))

    stages = (scores, exponentials, weighted_values, normalize)
    for t in range(len(units) + len(stages) - 1):
        for k, stage in enumerate(stages):
            if 0 <= t - k < len(units):
                stage(t - k)


def _attn_call(bias, qkv):
    B, r, L, W = qkv.shape[0], qkv.shape[1], qkv.shape[2], qkv.shape[3] // 3
    R = min(ATTN_STEP_ROWS, L)
    CB = min(r, ATTN_STEP_ROWS // R)
    cur = lambda width, part=0: pl.BlockSpec((None, CB, R, width), lambda b, g, i: (b, g, i, part))
    prev = lambda part: pl.BlockSpec((None, CB, BLK, W),
                                     lambda b, g, i: (b, g, jnp.maximum(i * (R // BLK) - 1, 0), part))
    return pl.pallas_call(
        _attn_kernel,
        grid=(B, r // CB, L // R),
        in_specs=[pl.BlockSpec(bias.shape, lambda b, g, i: (0, 0, 0)), cur(W, 0), cur(W, 1), cur(W, 2),
                  prev(1), prev(2)],
        out_specs=[cur(W), cur(LANES)],
        out_shape=[jax.ShapeDtypeStruct((B, r, L, W), BF16), jax.ShapeDtypeStruct((B, r, L, LANES), F32)],
        compiler_params=pltpu.CompilerParams(
            dimension_semantics=("arbitrary", "arbitrary", "arbitrary"), vmem_limit_bytes=VMEM_LIMIT),
        name=f"attn_d{r}",
    )(bias, qkv, qkv, qkv, qkv, qkv)


def _out_kernel(x_ref, p_ref, o1_ref, o4_ref, o16_ref, l1_ref, l4_ref, l16_ref, sg_ref, oa_ref,
                u4_ref, u16_ref, wo32_ref, gple_ref, wg32_ref, bg_ref, wp32_ref, out_ref, wo_ref, wg_ref, wp_ref):
    n_blocks = x_ref.shape[1] // PERM_ROWS

    @pl.when(jnp.logical_and(pl.program_id(0) == 0, pl.program_id(1) == 0))
    def _():
        for src, dst in ((wo32_ref, wo_ref), (wg32_ref, wg_ref), (wp32_ref, wp_ref)):
            for r in range(0, src.shape[0], MXU_DEPTH):
                dst[r:r + MXU_DEPTH, :] = src[r:r + MXU_DEPTH, :].astype(BF16)
    st = [dict() for _ in range(n_blocks)]

    def rows(b):
        return slice(b * PERM_ROWS, (b + 1) * PERM_ROWS)

    def grouped(ref, b):
        r = ref.shape[0]
        n = PERM_ROWS // r
        return jnp.concatenate([ref[c, b * n:(b + 1) * n, :] for c in range(r)], axis=0)

    def to_token_order(b):
        s = st[b]
        s["o4"] = _dot(u4_ref[...], grouped(o4_ref, b))
        s["o16"] = _dot(u16_ref[...], grouped(o16_ref, b))
        for name, ref, u_ref in (("l4", l4_ref, u4_ref), ("l16", l16_ref, u16_ref)):
            cm = grouped(ref, b)
            hi = cm.astype(BF16)
            lo = (cm - hi.astype(F32)).astype(BF16)
            t = _dot(u_ref[...], jnp.concatenate([hi, lo], axis=1))
            s[name] = t[:, :LANES] + t[:, LANES:]

    def merge_weights(b):
        s = st[b]
        l1 = l1_ref[0, rows(b), :]
        m = jnp.maximum(jnp.maximum(l1, s["l4"]), s["l16"])
        e1, e4, e16 = jnp.exp(l1 - m), jnp.exp(s["l4"] - m), jnp.exp(s["l16"] - m)
        inv = 1.0 / (e1 + e4 + e16)
        s["w"] = [e1 * inv, e4 * inv]

    lane = lax.broadcasted_iota(jnp.int32, (1, LANES), 1)

    def spread(w):
        pairs = []
        for p in range(N_PAIRS):
            t = w if p == 0 else pltpu.roll(w, LANES - STAT_REP * p, 1)
            for width in (STAT_REP, 2 * STAT_REP):
                t = jnp.where(lane % (2 * width) < width, t, pltpu.roll(t, width, 1))
            pairs.append(t)
        return jnp.concatenate(pairs, axis=1)

    def expand_weights(b):
        s = st[b]
        s["wf"] = [spread(w) for w in s["w"]]

    def merge(b):
        s = st[b]
        o1 = o1_ref[0, rows(b), :].astype(F32)
        ob = s["o16"] + s["wf"][0] * (o1 - s["o16"]) + s["wf"][1] * (s["o4"] - s["o16"])
        s["mb"] = (ob * sg_ref[0, rows(b), :].astype(F32)).astype(BF16)

    def project(b):
        s = st[b]
        s["t"] = _dot(s["mb"], wo_ref[:ATTN_WIDTH, :]) + _dot(oa_ref[0, rows(b), :], wo_ref[ATTN_WIDTH:, :])

    def residual_norm(b):
        s = st[b]
        h = x_ref[0, rows(b), :] + s["t"]
        s["h"] = h
        s["n"] = (h * lax.rsqrt(jnp.mean(h * h, axis=-1, keepdims=True) + EPS) * gple_ref[...]).astype(BF16)

    def embed_dots(b):
        s = st[b]
        s["g"] = _dot(s["n"], wg_ref[...])
        s["pp"] = _dot(p_ref[0, rows(b), :].astype(BF16), wp_ref[...])

    def embed_out(b):
        s = st[b]
        out_ref[0, rows(b), :] = s["h"] + jax.nn.sigmoid(s["g"] + bg_ref[...]) * s["pp"]

    stages = (to_token_order, merge_weights, expand_weights, merge, project, residual_norm, embed_dots, embed_out)
    on_matrix_unit = (to_token_order, project, embed_dots)
    for t in range(len(stages) + n_blocks - 1):
        ready = [(stages[t - b], b) for b in range(n_blocks) if 0 <= t - b < len(stages)]
        for stage, b in sorted(ready, key=lambda sb: sb[0] not in on_matrix_unit):
            stage(b)


def _out_call(x, p, os_, ls_, sg, oa, perms, w_out, g_ple, w_gate, b_gate, w_proj):
    B, S, D = x.shape
    T = OUT_ROWS
    tok = lambda width: pl.BlockSpec((1, T, width), lambda b, i: (b, i, 0))
    cls = lambda r, width: pl.BlockSpec((None, r, T // r, width), lambda b, i: (b, 0, i, 0))
    const = lambda shape: pl.BlockSpec(shape, lambda b, i: (0,) * len(shape))
    r4, r16 = STRIDES
    W = ATTN_WIDTH
    return pl.pallas_call(
        _out_kernel,
        grid=(B, S // T),
        in_specs=[tok(D), tok(PLE_DIM), tok(W), cls(r4, W), cls(r16, W), tok(LANES), cls(r4, LANES),
                  cls(r16, LANES), tok(W), tok(GMLP_WIDTH),
                  const((PERM_ROWS, PERM_ROWS)), const((PERM_ROWS, PERM_ROWS)),
                  const(w_out.shape), const((1, D)), const(w_gate.shape), const((1, D)), const(w_proj.shape)],
        out_specs=tok(D),
        out_shape=jax.ShapeDtypeStruct((B, S, D), x.dtype),
        scratch_shapes=[pltpu.VMEM(w.shape, BF16) for w in (w_out, w_gate, w_proj)],
        compiler_params=pltpu.CompilerParams(
            dimension_semantics=("arbitrary", "arbitrary"), vmem_limit_bytes=VMEM_LIMIT),
        name="out",
    )(x, p, *os_, *ls_, sg, oa, perms[r4][1], perms[r16][1], w_out, g_ple, w_gate, b_gate, w_proj)


def _band_bias():
    qi = np.arange(BLK)[:, None]
    col = np.arange(2 * BLK)[None, :]
    dist = qi + BLK - col
    band = (dist >= 0) & (dist <= BLK)
    both = np.stack([band, band & (col >= BLK)])
    bias = np.where(both, 0.0, NEG).astype(np.float32)
    return jnp.asarray(np.concatenate([bias, bias], axis=1))


def _head_average():
    h = np.arange(ATTN_WIDTH) // HEAD_DIM
    return jnp.asarray((h[:, None] == h[None, :]).astype(np.float32) / HEAD_DIM, dtype=BF16)


def _class_perms(r):
    t = np.arange(PERM_ROWS)
    grouped_pos = (t % r) * (PERM_ROWS // r) + t // r
    d = np.zeros((PERM_ROWS, PERM_ROWS), np.float32)
    d[grouped_pos, t] = 1.0
    return jnp.asarray(d, dtype=BF16), jnp.asarray(d.T, dtype=BF16)


def _inv_freq_col():
    half = HEAD_DIM // 2
    return jnp.exp(-math.log(ROPE_THETA) * jnp.arange(half, dtype=F32) / half)[:, None]


def kernel(x, p, positions, g_in, w_in, q_norm, k_norm, w_spatial, b_spatial, ln_v_g, ln_v_b, w_out,
           g_ple, w_ple_gate, b_ple_gate, w_ple_proj):
    B, S, _ = x.shape
    depth = w_in.shape[0]
    bias = _band_bias()
    avg = _head_average()
    inv_freq = _inv_freq_col()
    perms = {r: _class_perms(r) for r in STRIDES}
    pos_rows = positions.astype(F32).reshape(B, S // CHUNK, CHUNK)
    h = x
    for i in range(depth):
        qkv, qkv4, qkv16, sg, oa = _proj_call(
            h, pos_rows, inv_freq, g_in[i][None, :], w_in[i], avg,
            jnp.tile(q_norm[i], N_HEADS)[None, :], jnp.tile(k_norm[i], N_HEADS)[None, :],
            w_spatial[i], jnp.repeat(b_spatial[i].T, LANES, axis=1),
            ln_v_g[i][None, :], ln_v_b[i][None, :], perms)
        o1, l1 = _attn_call(bias, qkv[:, None])
        o4, l4 = _attn_call(bias, qkv4)
        o16, l16 = _attn_call(bias, qkv16)
        h = _out_call(h, p[i], (o1[:, 0], o4, o16), (l1[:, 0], l4, l16), sg, oa, perms,
                      w_out[i], g_ple[i][None, :], w_ple_gate[i], b_ple_gate[i][None, :], w_ple_proj[i])
    return h
```

```python
import functools
import math

import jax
import jax.numpy as jnp
import numpy as np
from jax import lax
from jax.experimental import pallas as pl
from jax.experimental.pallas import tpu as pltpu

D_MODEL = 1024
N_HEADS = 8
HEAD_DIM = 64
ATTN_WIDTH = N_HEADS * HEAD_DIM
DILATIONS = ((128, 1), (512, 4), (2048, 16))
BLK = 128
ROPE_THETA = 10000.0
GMLP_GROUPS = 4
GMLP_WIDTH = D_MODEL // 2
CHUNK = 128
PLE_DIM = 256
EPS = 1e-6
NEG = -1e30
LN2 = math.log(2.0)
LOG2E = math.log2(math.e)

LANES = 128
MXU_DEPTH = 256
N_PAIRS = ATTN_WIDTH // LANES
STAT_REP = LANES // N_HEADS
VMEM_LIMIT = 60 * 1024 * 1024

PROJ_ROWS = 512
ATTN_STEP_ROWS = 4096
OUT_ROWS = 1024
PERM_ROWS = MXU_DEPTH
STRIDES = tuple(d for _, d in DILATIONS if d > 1)

BF16 = jnp.bfloat16
F32 = jnp.float32


def _dot(a, b):
    return jnp.dot(a, b, preferred_element_type=F32)


def _dot_nt(a, b):
    return lax.dot_general(a, b, (((1,), (1,)), ((), ())), preferred_element_type=F32)


def _silu(x):
    return x * jax.nn.sigmoid(x)


def _project_pieces(x_ref, gin_ref, w_ref, z_ref):
    n_rows = x_ref.shape[1]
    half = n_rows // 2
    normed = {}

    def piece(h, j):
        rows = slice(h * half, (h + 1) * half)
        if h not in normed:
            x = x_ref[0, rows, :]
            scale = lax.rsqrt(jnp.mean(x * x, axis=-1, keepdims=True) + EPS)
            normed[h] = (x * scale * gin_ref[...]).astype(BF16)
        cols = slice(j * ATTN_WIDTH, (j + 1) * ATTN_WIDTH)
        z_ref[rows, cols] = _dot(normed[h], w_ref[:, cols])

    return [(lambda h=h, j=j: piece(h, j)) for h in range(2) for j in range(w_ref.shape[1] // ATTN_WIDTH)]


def _finish_pieces(first_block, z_ref, pos_ref, invf_ref, avg_ref, qg_ref, kg_ref, ws_ref, bs_ref, lng_ref, lnb_ref,
                   d4_ref, d16_ref, qkv_ref, qkv4_ref, qkv16_ref, sg_ref, oa_ref):
    n_rows = z_ref.shape[0]
    blocks = [slice(b * CHUNK, (b + 1) * CHUNK) for b in range(n_rows // CHUNK)]
    lane = lax.broadcasted_iota(jnp.int32, (1, LANES), 1)
    first_half = (lane % HEAD_DIM) < (HEAD_DIM // 2)
    cos, sin_signed, vn = {}, {}, {}

    def cols(j):
        return slice(j * ATTN_WIDTH, (j + 1) * ATTN_WIDTH)

    def tables(b):
        pos_row = pos_ref[0, pl.ds(first_block + b, 1), :]
        ang = pos_row * invf_ref[...]
        to_rows = lambda t: jnp.concatenate([t] * (LANES // t.shape[0]), axis=0).T
        cos[b] = to_rows(jnp.cos(ang))
        sin = to_rows(jnp.sin(ang))
        sin_signed[b] = jnp.where(first_half, -sin, sin)

    def norm_rope(b):
        zq, zk = z_ref[blocks[b], cols(0)], z_ref[blocks[b], cols(1)]
        sq = jnp.concatenate([(zq * zq).astype(BF16), (zk * zk).astype(BF16)], axis=0)
        ms = jnp.concatenate([_dot(sq[:, c:c + MXU_DEPTH], avg_ref[c:c + MXU_DEPTH, c:c + MXU_DEPTH])
                              for c in range(0, ATTN_WIDTH, MXU_DEPTH)], axis=1)
        for j, (z, g_ref, scale) in enumerate(((zq, qg_ref, HEAD_DIM ** -0.5 * LOG2E), (zk, kg_ref, 1.0))):
            y = z * lax.rsqrt(ms[j * CHUNK:(j + 1) * CHUNK] + EPS) * g_ref[...]
            outs = []
            for p in range(N_PAIRS):
                yp = y[:, p * LANES:(p + 1) * LANES]
                swapped = jnp.where(first_half, pltpu.roll(yp, LANES - HEAD_DIM // 2, 1),
                                    pltpu.roll(yp, HEAD_DIM // 2, 1))
                outs.append(yp * cos[b] + swapped * sin_signed[b])
            r = jnp.concatenate(outs, axis=1)
            if scale != 1.0:
                r = r * scale
            qkv_ref[0, blocks[b], cols(j)] = r.astype(BF16)

    def value(b):
        qkv_ref[0, blocks[b], cols(2)] = z_ref[blocks[b], cols(2)].astype(BF16)

    def regroup(j, part):
        piece = qkv_ref[0, j * PERM_ROWS:(j + 1) * PERM_ROWS, cols(part)]
        for (d_ref, out_ref) in ((d4_ref, qkv4_ref), (d16_ref, qkv16_ref)):
            r = out_ref.shape[1]
            n = PERM_ROWS // r
            cm = _dot(d_ref[...], piece).astype(BF16)
            for c in range(r):
                out_ref[0, c, j * n:(j + 1) * n, cols(part)] = cm[c * n:(c + 1) * n]

    def gate(b):
        sg_ref[0, blocks[b], :] = _silu(z_ref[blocks[b], cols(3)]).astype(BF16)

    def gmlp_pre(b):
        gv = jax.nn.gelu(z_ref[blocks[b], cols(5)])
        mu = jnp.mean(gv, axis=-1, keepdims=True)
        gc = gv - mu
        var = jnp.mean(gc * gc, axis=-1, keepdims=True)
        vn[b] = (gc * lax.rsqrt(var + EPS) * lng_ref[...] + lnb_ref[...]).astype(BF16)

    def gmlp_post(b):
        r_i = lax.broadcasted_iota(jnp.int32, (CHUNK, CHUNK), 0)
        c_i = lax.broadcasted_iota(jnp.int32, (CHUNK, CHUNK), 1)
        sv = jnp.concatenate(
            [_dot(jnp.where(c_i <= r_i, ws_ref[g], 0.0).astype(BF16), vn[b][:, g * LANES:(g + 1) * LANES])
             for g in range(GMLP_GROUPS)], axis=1) + bs_ref[...]
        u = jax.nn.gelu(z_ref[blocks[b], cols(4)])
        ga = _silu(z_ref[blocks[b], cols(6)])
        oa_ref[0, blocks[b], :] = (u * sv * ga).astype(BF16)

    nb = range(len(blocks))
    nc = range(n_rows // PERM_ROWS)
    pieces = []
    for b in nb:
        pieces += [(2.5, lambda b=b: tables(b)),
                   (6.0, lambda b=b: norm_rope(b)),
                   (0.5, lambda b=b: value(b))]
    pieces += [(1.0, lambda b=b: gate(b)) for b in nb]
    pieces += [(2.0, lambda j=j: regroup(j, 0)) for j in nc]
    pieces += [(3.5, lambda b=b: gmlp_pre(b)) for b in nb]
    pieces += [(2.0, lambda j=j: regroup(j, 1)) for j in nc]
    pieces += [(2.0, lambda j=j: regroup(j, 2)) for j in nc]
    pieces += [(4.0, lambda b=b: gmlp_post(b)) for b in nb]
    return pieces


def _interleave(weighted, uniform):
    total = sum(w for w, _ in weighted)
    out, taken, done = [], 0, 0.0
    for w, piece in weighted:
        while taken < len(uniform) and done >= (taken + 0.5) * total / len(uniform):
            out.append(uniform[taken])
            taken += 1
        out.append(piece)
        done += w
    return out + uniform[taken:]


def _proj_kernel(tiles_per_seq, x_ref, pos_ref, invf_ref, gin_ref, w_ref, *rest):
    finish_refs, (z0_ref, z1_ref, wb_ref) = rest[:-3], rest[-3:]
    step = pl.program_id(0)
    first_block = (jnp.maximum(step - 1, 0) % tiles_per_seq) * (x_ref.shape[1] // CHUNK)

    @pl.when(step == 0)
    def _():
        z1_ref[...] = jnp.zeros_like(z1_ref)
        for c in range(0, w_ref.shape[1], ATTN_WIDTH):
            wb_ref[:, c:c + ATTN_WIDTH] = w_ref[:, c:c + ATTN_WIDTH].astype(BF16)

    def run(z_write_ref, z_read_ref):
        for piece in _interleave(_finish_pieces(first_block, z_read_ref, pos_ref, invf_ref, *finish_refs),
                                 _project_pieces(x_ref, gin_ref, wb_ref, z_write_ref)):
            piece()

    @pl.when(step % 2 == 0)
    def _():
        run(z0_ref, z1_ref)

    @pl.when(step % 2 == 1)
    def _():
        run(z1_ref, z0_ref)


def _proj_call(x, pos_rows, inv_freq, g_in, w_in, avg, qg, kg, w_s, b_tab, ln_g, ln_b, perms):
    B, S, D = x.shape
    T = PROJ_ROWS
    n_cols = w_in.shape[1]
    W = ATTN_WIDTH
    per_seq = S // T
    n_tiles = B * per_seq
    proj_tile = lambda s: jnp.minimum(s, n_tiles - 1)
    done_tile = lambda s: jnp.maximum(s - 1, 0)
    tok_in = pl.BlockSpec((1, T, D), lambda s: (proj_tile(s) // per_seq, proj_tile(s) % per_seq, 0))
    tok = lambda width: pl.BlockSpec((1, T, width), lambda s: (done_tile(s) // per_seq, done_tile(s) % per_seq, 0))
    cls = lambda r: pl.BlockSpec((1, r, T // r, 3 * W),
                                 lambda s: (done_tile(s) // per_seq, 0, done_tile(s) % per_seq, 0))
    seq_pos = pl.BlockSpec((1, S // CHUNK, CHUNK), lambda s: (done_tile(s) // per_seq, 0, 0))
    const = lambda shape: pl.BlockSpec(shape, lambda s: (0,) * len(shape))
    nat_sds = lambda width: jax.ShapeDtypeStruct((B, S, width), BF16)
    cls_sds = lambda r: jax.ShapeDtypeStruct((B, r, S // r, 3 * W), BF16)
    r4, r16 = STRIDES
    return pl.pallas_call(
        functools.partial(_proj_kernel, per_seq),
        grid=(n_tiles + 1,),
        scratch_shapes=[pltpu.VMEM((T, n_cols), F32), pltpu.VMEM((T, n_cols), F32), pltpu.VMEM((D, n_cols), BF16)],
        in_specs=[tok_in, seq_pos, const(inv_freq.shape), const((1, D)), const((D, n_cols)),
                  const((W, W)), const((1, W)), const((1, W)),
                  const((GMLP_GROUPS, CHUNK, CHUNK)), const((CHUNK, GMLP_WIDTH)),
                  const((1, GMLP_WIDTH)), const((1, GMLP_WIDTH)),
                  const((PERM_ROWS, PERM_ROWS)), const((PERM_ROWS, PERM_ROWS))],
        out_specs=[tok(3 * W), cls(r4), cls(r16), tok(W), tok(W)],
        out_shape=[nat_sds(3 * W), cls_sds(r4), cls_sds(r16), nat_sds(W), nat_sds(W)],
        compiler_params=pltpu.CompilerParams(dimension_semantics=("arbitrary",), vmem_limit_bytes=VMEM_LIMIT),
        name="proj",
    )(x, pos_rows, inv_freq, g_in, w_in, avg, qg, kg, w_s, b_tab, ln_g, ln_b, perms[r4][0], perms[r16][0])


def _attn_kernel(bias_ref, q_ref, k_ref, v_ref, kp_ref, vp_ref, o_ref, l_ref):
    n_cls, n_rows = q_ref.shape[0], q_ref.shape[1]
    first_tile = pl.program_id(2) == 0
    lane = lax.broadcasted_iota(jnp.int32, (1, LANES), 1)
    in_a = lane < HEAD_DIM
    group = lane // STAT_REP % N_PAIRS
    ones = jnp.ones((2 * BLK, LANES), BF16)
    blocks_per_cls = n_rows // BLK
    units = [(c, s, p) for c in range(n_cls) for s in range(blocks_per_cls) for p in range(N_PAIRS)]
    st = [dict() for _ in units]
    packed = {}

    def lanes_of(p):
        return slice(p * LANES, (p + 1) * LANES)

    def window(cur_ref, prev_ref, c, s, p):
        if s == 0:
            return jnp.concatenate([prev_ref[c, :, lanes_of(p)], cur_ref[c, :BLK, lanes_of(p)]], axis=0)
        return cur_ref[c, (s - 1) * BLK:(s + 1) * BLK, lanes_of(p)]

    def scores(u):
        c, s, p = units[u]
        qp = q_ref[c, s * BLK:(s + 1) * BLK, lanes_of(p)]
        zero = jnp.zeros_like(qp)
        q2 = jnp.concatenate([jnp.where(in_a, qp, zero), jnp.where(in_a, zero, qp)], axis=0)
        st[u]["sc"] = _dot_nt(q2, window(k_ref, kp_ref, c, s, p))

    def exponentials(u):
        _, s, _ = units[u]
        bias = bias_ref[first_tile.astype(jnp.int32)] if s == 0 else bias_ref[0]
        sc = st[u].pop("sc") + bias
        m = jnp.max(sc, axis=-1, keepdims=True)
        st[u]["m"] = m
        st[u]["e"] = jnp.exp2((sc - m).astype(BF16))

    def weighted_values(u):
        c, s, p = units[u]
        vv = window(v_ref, vp_ref, c, s, p)
        st[u]["res"] = _dot(st[u].pop("e"), jnp.concatenate([vv, ones], axis=1))

    def normalize(u):
        c, s, p = units[u]
        res, m = st[u].pop("res"), st[u].pop("m")
        num = jnp.where(in_a, res[:BLK, :LANES], res[BLK:, :LANES])
        den = jnp.where(in_a, res[:BLK, LANES:], res[BLK:, LANES:])
        o_ref[c, s * BLK:(s + 1) * BLK, lanes_of(p)] = (num / den).astype(o_ref.dtype)
        lse = jnp.where(in_a, m[:BLK], m[BLK:]) * LN2 + jnp.log(den)
        packed[c, s] = lse if p == 0 else jnp.where(group == p, lse, packed[c, s])
        if p == N_PAIRS - 1:
            l_ref[c, s * BLK:(s + 1) * BLK, :] = packed.pop((c, s))

    stages = (scores, exponentials, weighted_values, normalize)
    for t in range(len(units) + len(stages) - 1):
        for k, stage in enumerate(stages):
            if 0 <= t - k < len(units):
                stage(t - k)


def _attn_call(bias, qkv):
    B, r, L, W = qkv.shape[0], qkv.shape[1], qkv.shape[2], qkv.shape[3] // 3
    R = min(ATTN_STEP_ROWS, L)
    CB = min(r, ATTN_STEP_ROWS // R)
    cur = lambda width, part=0: pl.BlockSpec((None, CB, R, width), lambda b, g, i: (b, g, i, part))
    prev = lambda part: pl.BlockSpec((None, CB, BLK, W),
                                     lambda b, g, i: (b, g, jnp.maximum(i * (R // BLK) - 1, 0), part))
    return pl.pallas_call(
        _attn_kernel,
        grid=(B, r // CB, L // R),
        in_specs=[pl.BlockSpec(bias.shape, lambda b, g, i: (0, 0, 0)), cur(W, 0), cur(W, 1), cur(W, 2),
                  prev(1), prev(2)],
        out_specs=[cur(W), cur(LANES)],
        out_shape=[jax.ShapeDtypeStruct((B, r, L, W), BF16), jax.ShapeDtypeStruct((B, r, L, LANES), F32)],
        compiler_params=pltpu.CompilerParams(
            dimension_semantics=("arbitrary", "arbitrary", "arbitrary"), vmem_limit_bytes=VMEM_LIMIT),
        name=f"attn_d{r}",
    )(bias, qkv, qkv, qkv, qkv, qkv)


def _out_kernel(x_ref, p_ref, o1_ref, o4_ref, o16_ref, l1_ref, l4_ref, l16_ref, sg_ref, oa_ref,
                u4_ref, u16_ref, wo32_ref, gple_ref, wg32_ref, bg_ref, wp32_ref, out_ref, wo_ref, wg_ref, wp_ref):
    n_blocks = x_ref.shape[1] // PERM_ROWS

    @pl.when(jnp.logical_and(pl.program_id(0) == 0, pl.program_id(1) == 0))
    def _():
        for src, dst in ((wo32_ref, wo_ref), (wg32_ref, wg_ref), (wp32_ref, wp_ref)):
            for r in range(0, src.shape[0], MXU_DEPTH):
                dst[r:r + MXU_DEPTH, :] = src[r:r + MXU_DEPTH, :].astype(BF16)
    st = [dict() for _ in range(n_blocks)]

    def rows(b):
        return slice(b * PERM_ROWS, (b + 1) * PERM_ROWS)

    def grouped(ref, b):
        r = ref.shape[0]
        n = PERM_ROWS // r
        return jnp.concatenate([ref[c, b * n:(b + 1) * n, :] for c in range(r)], axis=0)

    def to_token_order(b):
        s = st[b]
        s["o4"] = _dot(u4_ref[...], grouped(o4_ref, b))
        s["o16"] = _dot(u16_ref[...], grouped(o16_ref, b))
        for name, ref, u_ref in (("l4", l4_ref, u4_ref), ("l16", l16_ref, u16_ref)):
            cm = grouped(ref, b)
            hi = cm.astype(BF16)
            lo = (cm - hi.astype(F32)).astype(BF16)
            t = _dot(u_ref[...], jnp.concatenate([hi, lo], axis=1))
            s[name] = t[:, :LANES] + t[:, LANES:]

    def merge_weights(b):
        s = st[b]
        l1 = l1_ref[0, rows(b), :]
        m = jnp.maximum(jnp.maximum(l1, s["l4"]), s["l16"])
        e1, e4, e16 = jnp.exp(l1 - m), jnp.exp(s["l4"] - m), jnp.exp(s["l16"] - m)
        inv = 1.0 / (e1 + e4 + e16)
        s["w"] = [e1 * inv, e4 * inv]

    lane = lax.broadcasted_iota(jnp.int32, (1, LANES), 1)

    def spread(w):
        pairs = []
        for p in range(N_PAIRS):
            t = w if p == 0 else pltpu.roll(w, LANES - STAT_REP * p, 1)
            for width in (STAT_REP, 2 * STAT_REP):
                t = jnp.where(lane % (2 * width) < width, t, pltpu.roll(t, width, 1))
            pairs.append(t)
        return jnp.concatenate(pairs, axis=1)

    def expand_weights(b):
        s = st[b]
        s["wf"] = [spread(w) for w in s["w"]]

    def merge(b):
        s = st[b]
        o1 = o1_ref[0, rows(b), :].astype(F32)
        ob = s["o16"] + s["wf"][0] * (o1 - s["o16"]) + s["wf"][1] * (s["o4"] - s["o16"])
        s["mb"] = (ob * sg_ref[0, rows(b), :].astype(F32)).astype(BF16)

    def project(b):
        s = st[b]
        s["t"] = _dot(s["mb"], wo_ref[:ATTN_WIDTH, :]) + _dot(oa_ref[0, rows(b), :], wo_ref[ATTN_WIDTH:, :])

    def residual_norm(b):
        s = st[b]
        h = x_ref[0, rows(b), :] + s["t"]
        s["h"] = h
        s["n"] = (h * lax.rsqrt(jnp.mean(h * h, axis=-1, keepdims=True) + EPS) * gple_ref[...]).astype(BF16)

    def embed_dots(b):
        s = st[b]
        s["g"] = _dot(s["n"], wg_ref[...])
        s["pp"] = _dot(p_ref[0, rows(b), :].astype(BF16), wp_ref[...])

    def embed_out(b):
        s = st[b]
        out_ref[0, rows(b), :] = s["h"] + jax.nn.sigmoid(s["g"] + bg_ref[...]) * s["pp"]

    stages = (to_token_order, merge_weights, expand_weights, merge, project, residual_norm, embed_dots, embed_out)
    on_matrix_unit = (to_token_order, project, embed_dots)
    for t in range(len(stages) + n_blocks - 1):
        ready = [(stages[t - b], b) for b in range(n_blocks) if 0 <= t - b < len(stages)]
        for stage, b in sorted(ready, key=lambda sb: sb[0] not in on_matrix_unit):
            stage(b)


def _out_call(x, p, os_, ls_, sg, oa, perms, w_out, g_ple, w_gate, b_gate, w_proj):
    B, S, D = x.shape
    T = OUT_ROWS
    tok = lambda width: pl.BlockSpec((1, T, width), lambda b, i: (b, i, 0))
    cls = lambda r, width: pl.BlockSpec((None, r, T // r, width), lambda b, i: (b, 0, i, 0))
    const = lambda shape: pl.BlockSpec(shape, lambda b, i: (0,) * len(shape))
    r4, r16 = STRIDES
    W = ATTN_WIDTH
    return pl.pallas_call(
        _out_kernel,
        grid=(B, S // T),
        in_specs=[tok(D), tok(PLE_DIM), tok(W), cls(r4, W), cls(r16, W), tok(LANES), cls(r4, LANES),
                  cls(r16, LANES), tok(W), tok(GMLP_WIDTH),
                  const((PERM_ROWS, PERM_ROWS)), const((PERM_ROWS, PERM_ROWS)),
                  const(w_out.shape), const((1, D)), const(w_gate.shape), const((1, D)), const(w_proj.shape)],
        out_specs=tok(D),
        out_shape=jax.ShapeDtypeStruct((B, S, D), x.dtype),
        scratch_shapes=[pltpu.VMEM(w.shape, BF16) for w in (w_out, w_gate, w_proj)],
        compiler_params=pltpu.CompilerParams(
            dimension_semantics=("arbitrary", "arbitrary"), vmem_limit_bytes=VMEM_LIMIT),
        name="out",
    )(x, p, *os_, *ls_, sg, oa, perms[r4][1], perms[r16][1], w_out, g_ple, w_gate, b_gate, w_proj)


def _band_bias():
    qi = np.arange(BLK)[:, None]
    col = np.arange(2 * BLK)[None, :]
    dist = qi + BLK - col
    band = (dist >= 0) & (dist <= BLK)
    both = np.stack([band, band & (col >= BLK)])
    bias = np.where(both, 0.0, NEG).astype(np.float32)
    return jnp.asarray(np.concatenate([bias, bias], axis=1))


def _head_average():
    h = np.arange(ATTN_WIDTH) // HEAD_DIM
    return jnp.asarray((h[:, None] == h[None, :]).astype(np.float32) / HEAD_DIM, dtype=BF16)


def _class_perms(r):
    t = np.arange(PERM_ROWS)
    grouped_pos = (t % r) * (PERM_ROWS // r) + t // r
    d = np.zeros((PERM_ROWS, PERM_ROWS), np.float32)
    d[grouped_pos, t] = 1.0
    return jnp.asarray(d, dtype=BF16), jnp.asarray(d.T, dtype=BF16)


def _inv_freq_col():
    half = HEAD_DIM // 2
    return jnp.exp(-math.log(ROPE_THETA) * jnp.arange(half, dtype=F32) / half)[:, None]


def kernel(x, p, positions, g_in, w_in, q_norm, k_norm, w_spatial, b_spatial, ln_v_g, ln_v_b, w_out,
           g_ple, w_ple_gate, b_ple_gate, w_ple_proj):
    B, S, _ = x.shape
    depth = w_in.shape[0]
    bias = _band_bias()
    avg = _head_average()
    inv_freq = _inv_freq_col()
    perms = {r: _class_perms(r) for r in STRIDES}
    pos_rows = positions.astype(F32).reshape(B, S // CHUNK, CHUNK)
    h = x
    for i in range(depth):
        qkv, qkv4, qkv16, sg, oa = _proj_call(
            h, pos_rows, inv_freq, g_in[i][None, :], w_in[i], avg,
            jnp.tile(q_norm[i], N_HEADS)[None, :], jnp.tile(k_norm[i], N_HEADS)[None, :],
            w_spatial[i], jnp.repeat(b_spatial[i].T, LANES, axis=1),
            ln_v_g[i][None, :], ln_v_b[i][None, :], perms)
        o1, l1 = _attn_call(bias, qkv[:, None])
        o4, l4 = _attn_call(bias, qkv4)
        o16, l16 = _attn_call(bias, qkv16)
        h = _out_call(h, p[i], (o1[:, 0], o4, o16), (l1[:, 0], l4, l16), sg, oa, perms,
                      w_out[i], g_ple[i][None, :], w_ple_gate[i], b_ple_gate[i][None, :], w_ple_proj[i])
    return h
```

```python
import functools
import math

import jax
import jax.numpy as jnp
import numpy as np
from jax import lax
from jax.experimental import pallas as pl
from jax.experimental.pallas import tpu as pltpu

D_MODEL = 1024
N_HEADS = 8
HEAD_DIM = 64
ATTN_WIDTH = N_HEADS * HEAD_DIM
DILATIONS = ((128, 1), (512, 4), (2048, 16))
BLK = 128
ROPE_THETA = 10000.0
GMLP_GROUPS = 4
GMLP_WIDTH = D_MODEL // 2
CHUNK = 128
PLE_DIM = 256
EPS = 1e-6
NEG = -1e30
LN2 = math.log(2.0)
LOG2E = math.log2(math.e)

LANES = 128
MXU_DEPTH = 256
N_PAIRS = ATTN_WIDTH // LANES
STAT_REP = LANES // N_HEADS
VMEM_LIMIT = 60 * 1024 * 1024

PROJ_ROWS = 512
ATTN_STEP_ROWS = 4096
OUT_ROWS = 1024
PERM_ROWS = MXU_DEPTH
STRIDES = tuple(d for _, d in DILATIONS if d > 1)

BF16 = jnp.bfloat16
F32 = jnp.float32


def _dot(a, b):
    return jnp.dot(a, b, preferred_element_type=F32)


def _dot_nt(a, b):
    return lax.dot_general(a, b, (((1,), (1,)), ((), ())), preferred_element_type=F32)


def _silu(x):
    return x * jax.nn.sigmoid(x)


def _project_pieces(x_ref, gin_ref, w_ref, z_ref):
    n_rows = x_ref.shape[1]
    half = n_rows // 2
    normed = {}

    def piece(h, j):
        rows = slice(h * half, (h + 1) * half)
        if h not in normed:
            x = x_ref[0, rows, :]
            scale = lax.rsqrt(jnp.mean(x * x, axis=-1, keepdims=True) + EPS)
            normed[h] = (x * scale * gin_ref[...]).astype(BF16)
        cols = slice(j * ATTN_WIDTH, (j + 1) * ATTN_WIDTH)
        z_ref[rows, cols] = _dot(normed[h], w_ref[:, cols])

    return [(lambda h=h, j=j: piece(h, j)) for h in range(2) for j in range(w_ref.shape[1] // ATTN_WIDTH)]


def _finish_pieces(first_block, z_ref, pos_ref, invf_ref, avg_ref, qg_ref, kg_ref, ws_ref, bs_ref, lng_ref, lnb_ref,
                   d4_ref, d16_ref, qkv_ref, qkv4_ref, qkv16_ref, sg_ref, oa_ref):
    n_rows = z_ref.shape[0]
    blocks = [slice(b * CHUNK, (b + 1) * CHUNK) for b in range(n_rows // CHUNK)]
    lane = lax.broadcasted_iota(jnp.int32, (1, LANES), 1)
    first_half = (lane % HEAD_DIM) < (HEAD_DIM // 2)
    cos, sin_signed, vn = {}, {}, {}

    def cols(j):
        return slice(j * ATTN_WIDTH, (j + 1) * ATTN_WIDTH)

    def tables(b):
        pos_row = pos_ref[0, pl.ds(first_block + b, 1), :]
        ang = pos_row * invf_ref[...]
        to_rows = lambda t: jnp.concatenate([t] * (LANES // t.shape[0]), axis=0).T
        cos[b] = to_rows(jnp.cos(ang))
        sin = to_rows(jnp.sin(ang))
        sin_signed[b] = jnp.where(first_half, -sin, sin)

    def norm_rope(b):
        zq, zk = z_ref[blocks[b], cols(0)], z_ref[blocks[b], cols(1)]
        sq = jnp.concatenate([(zq * zq).astype(BF16), (zk * zk).astype(BF16)], axis=0)
        ms = jnp.concatenate([_dot(sq[:, c:c + MXU_DEPTH], avg_ref[c:c + MXU_DEPTH, c:c + MXU_DEPTH])
                              for c in range(0, ATTN_WIDTH, MXU_DEPTH)], axis=1)
        for j, (z, g_ref, scale) in enumerate(((zq, qg_ref, HEAD_DIM ** -0.5 * LOG2E), (zk, kg_ref, 1.0))):
            y = z * lax.rsqrt(ms[j * CHUNK:(j + 1) * CHUNK] + EPS) * g_ref[...]
            outs = []
            for p in range(N_PAIRS):
                yp = y[:, p * LANES:(p + 1) * LANES]
                swapped = jnp.where(first_half, pltpu.roll(yp, LANES - HEAD_DIM // 2, 1),
                                    pltpu.roll(yp, HEAD_DIM // 2, 1))
                outs.append(yp * cos[b] + swapped * sin_signed[b])
            r = jnp.concatenate(outs, axis=1)
            if scale != 1.0:
                r = r * scale
            qkv_ref[0, blocks[b], cols(j)] = r.astype(BF16)

    def value(b):
        qkv_ref[0, blocks[b], cols(2)] = z_ref[blocks[b], cols(2)].astype(BF16)

    def regroup(j, part):
        piece = qkv_ref[0, j * PERM_ROWS:(j + 1) * PERM_ROWS, cols(part)]
        for (d_ref, out_ref) in ((d4_ref, qkv4_ref), (d16_ref, qkv16_ref)):
            r = out_ref.shape[1]
            n = PERM_ROWS // r
            cm = _dot(d_ref[...], piece).astype(BF16)
            for c in range(r):
                out_ref[0, c, j * n:(j + 1) * n, cols(part)] = cm[c * n:(c + 1) * n]

    def gate(b):
        sg_ref[0, blocks[b], :] = _silu(z_ref[blocks[b], cols(3)]).astype(BF16)

    def gmlp_pre(b):
        gv = jax.nn.gelu(z_ref[blocks[b], cols(5)])
        mu = jnp.mean(gv, axis=-1, keepdims=True)
        gc = gv - mu
        var = jnp.mean(gc * gc, axis=-1, keepdims=True)
        vn[b] = (gc * lax.rsqrt(var + EPS) * lng_ref[...] + lnb_ref[...]).astype(BF16)

    def gmlp_post(b):
        r_i = lax.broadcasted_iota(jnp.int32, (CHUNK, CHUNK), 0)
        c_i = lax.broadcasted_iota(jnp.int32, (CHUNK, CHUNK), 1)
        sv = jnp.concatenate(
            [_dot(jnp.where(c_i <= r_i, ws_ref[g], 0.0).astype(BF16), vn[b][:, g * LANES:(g + 1) * LANES])
             for g in range(GMLP_GROUPS)], axis=1) + bs_ref[...]
        u = jax.nn.gelu(z_ref[blocks[b], cols(4)])
        ga = _silu(z_ref[blocks[b], cols(6)])
        oa_ref[0, blocks[b], :] = (u * sv * ga).astype(BF16)

    nb = range(len(blocks))
    nc = range(n_rows // PERM_ROWS)
    pieces = []
    for b in nb:
        pieces += [(2.5, lambda b=b: tables(b)),
                   (6.0, lambda b=b: norm_rope(b)),
                   (0.5, lambda b=b: value(b))]
    pieces += [(1.0, lambda b=b: gate(b)) for b in nb]
    pieces += [(2.0, lambda j=j: regroup(j, 0)) for j in nc]
    pieces += [(3.5, lambda b=b: gmlp_pre(b)) for b in nb]
    pieces += [(2.0, lambda j=j: regroup(j, 1)) for j in nc]
    pieces += [(2.0, lambda j=j: regroup(j, 2)) for j in nc]
    pieces += [(4.0, lambda b=b: gmlp_post(b)) for b in nb]
    return pieces


def _interleave(weighted, uniform):
    total = sum(w for w, _ in weighted)
    out, taken, done = [], 0, 0.0
    for w, piece in weighted:
        while taken < len(uniform) and done >= (taken + 0.5) * total / len(uniform):
            out.append(uniform[taken])
            taken += 1
        out.append(piece)
        done += w
    return out + uniform[taken:]


def _proj_kernel(tiles_per_seq, x_ref, pos_ref, invf_ref, gin_ref, w_ref, *rest):
    finish_refs, (z0_ref, z1_ref, wb_ref) = rest[:-3], rest[-3:]
    step = pl.program_id(0)
    first_block = (jnp.maximum(step - 1, 0) % tiles_per_seq) * (x_ref.shape[1] // CHUNK)

    @pl.when(step == 0)
    def _():
        z1_ref[...] = jnp.zeros_like(z1_ref)
        for c in range(0, w_ref.shape[1], ATTN_WIDTH):
            wb_ref[:, c:c + ATTN_WIDTH] = w_ref[:, c:c + ATTN_WIDTH].astype(BF16)

    def run(z_write_ref, z_read_ref):
        for piece in _interleave(_finish_pieces(first_block, z_read_ref, pos_ref, invf_ref, *finish_refs),
                                 _project_pieces(x_ref, gin_ref, wb_ref, z_write_ref)):
            piece()

    @pl.when(step % 2 == 0)
    def _():
        run(z0_ref, z1_ref)

    @pl.when(step % 2 == 1)
    def _():
        run(z1_ref, z0_ref)


def _proj_call(x, pos_rows, inv_freq, g_in, w_in, avg, qg, kg, w_s, b_tab, ln_g, ln_b, perms):
    B, S, D = x.shape
    T = PROJ_ROWS
    n_cols = w_in.shape[1]
    W = ATTN_WIDTH
    per_seq = S // T
    n_tiles = B * per_seq
    proj_tile = lambda s: jnp.minimum(s, n_tiles - 1)
    done_tile = lambda s: jnp.maximum(s - 1, 0)
    tok_in = pl.BlockSpec((1, T, D), lambda s: (proj_tile(s) // per_seq, proj_tile(s) % per_seq, 0))
    tok = lambda width: pl.BlockSpec((1, T, width), lambda s: (done_tile(s) // per_seq, done_tile(s) % per_seq, 0))
    cls = lambda r: pl.BlockSpec((1, r, T // r, 3 * W),
                                 lambda s: (done_tile(s) // per_seq, 0, done_tile(s) % per_seq, 0))
    seq_pos = pl.BlockSpec((1, S // CHUNK, CHUNK), lambda s: (done_tile(s) // per_seq, 0, 0))
    const = lambda shape: pl.BlockSpec(shape, lambda s: (0,) * len(shape))
    nat_sds = lambda width: jax.ShapeDtypeStruct((B, S, width), BF16)
    cls_sds = lambda r: jax.ShapeDtypeStruct((B, r, S // r, 3 * W), BF16)
    r4, r16 = STRIDES
    return pl.pallas_call(
        functools.partial(_proj_kernel, per_seq),
        grid=(n_tiles + 1,),
        scratch_shapes=[pltpu.VMEM((T, n_cols), F32), pltpu.VMEM((T, n_cols), F32), pltpu.VMEM((D, n_cols), BF16)],
        in_specs=[tok_in, seq_pos, const(inv_freq.shape), const((1, D)), const((D, n_cols)),
                  const((W, W)), const((1, W)), const((1, W)),
                  const((GMLP_GROUPS, CHUNK, CHUNK)), const((CHUNK, GMLP_WIDTH)),
                  const((1, GMLP_WIDTH)), const((1, GMLP_WIDTH)),
                  const((PERM_ROWS, PERM_ROWS)), const((PERM_ROWS, PERM_ROWS))],
        out_specs=[tok(3 * W), cls(r4), cls(r16), tok(W), tok(W)],
        out_shape=[nat_sds(3 * W), cls_sds(r4), cls_sds(r16), nat_sds(W), nat_sds(W)],
        compiler_params=pltpu.CompilerParams(dimension_semantics=("arbitrary",), vmem_limit_bytes=VMEM_LIMIT),
        name="proj",
    )(x, pos_rows, inv_freq, g_in, w_in, avg, qg, kg, w_s, b_tab, ln_g, ln_b, perms[r4][0], perms[r16][0])


def _attn_kernel(bias_ref, q_ref, k_ref, v_ref, kp_ref, vp_ref, o_ref, l_ref):
    n_cls, n_rows = q_ref.shape[0], q_ref.shape[1]
    first_tile = pl.program_id(2) == 0
    lane = lax.broadcasted_iota(jnp.int32, (1, LANES), 1)
    in_a = lane < HEAD_DIM
    group = lane // STAT_REP % N_PAIRS
    ones = jnp.ones((2 * BLK, LANES), BF16)
    blocks_per_cls = n_rows // BLK
    units = [(c, s, p) for c in range(n_cls) for s in range(blocks_per_cls) for p in range(N_PAIRS)]
    st = [dict() for _ in units]
    packed = {}

    def lanes_of(p):
        return slice(p * LANES, (p + 1) * LANES)

    def window(cur_ref, prev_ref, c, s, p):
        if s == 0:
            return jnp.concatenate([prev_ref[c, :, lanes_of(p)], cur_ref[c, :BLK, lanes_of(p)]], axis=0)
        return cur_ref[c, (s - 1) * BLK:(s + 1) * BLK, lanes_of(p)]

    def scores(u):
        c, s, p = units[u]
        qp = q_ref[c, s * BLK:(s + 1) * BLK, lanes_of(p)]
        zero = jnp.zeros_like(qp)
        q2 = jnp.concatenate([jnp.where(in_a, qp, zero), jnp.where(in_a, zero, qp)], axis=0)
        st[u]["sc"] = _dot_nt(q2, window(k_ref, kp_ref, c, s, p))

    def exponentials(u):
        _, s, _ = units[u]
        bias = bias_ref[first_tile.astype(jnp.int32)] if s == 0 else bias_ref[0]
        sc = st[u].pop("sc") + bias
        m = jnp.max(sc, axis=-1, keepdims=True)
        st[u]["m"] = m
        st[u]["e"] = jnp.exp2(sc - m).astype(BF16)

    def weighted_values(u):
        c, s, p = units[u]
        vv = window(v_ref, vp_ref, c, s, p)
        st[u]["res"] = _dot(st[u].pop("e"), jnp.concatenate([vv, ones], axis=1))

    def normalize(u):
        c, s, p = units[u]
        res, m = st[u].pop("res"), st[u].pop("m")
        num = jnp.where(in_a, res[:BLK, :LANES], res[BLK:, :LANES])
        den = jnp.where(in_a, res[:BLK, LANES:], res[BLK:, LANES:])
        o_ref[c, s * BLK:(s + 1) * BLK, lanes_of(p)] = (num / den).astype(o_ref.dtype)
        lse = jnp.where(in_a, m[:BLK], m[BLK:]) * LN2 + jnp.log(den)
        packed[c, s] = lse if p == 0 else jnp.where(group == p, lse, packed[c, s])
        if p == N_PAIRS - 1:
            l_ref[c, s * BLK:(s + 1) * BLK, :] = packed.pop((c, s))

    stages = (scores, exponentials, weighted_values, normalize)
    for t in range(len(units) + len(stages) - 1):
        for k, stage in enumerate(stages):
            if 0 <= t - k < len(units):
                stage(t - k)


def _attn_call(bias, qkv):
    B, r, L, W = qkv.shape[0], qkv.shape[1], qkv.shape[2], qkv.shape[3] // 3
    R = min(ATTN_STEP_ROWS, L)
    CB = min(r, ATTN_STEP_ROWS // R)
    cur = lambda width, part=0: pl.BlockSpec((None, CB, R, width), lambda b, g, i: (b, g, i, part))
    prev = lambda part: pl.BlockSpec((None, CB, BLK, W),
                                     lambda b, g, i: (b, g, jnp.maximum(i * (R // BLK) - 1, 0), part))
    return pl.pallas_call(
        _attn_kernel,
        grid=(B, r // CB, L // R),
        in_specs=[pl.BlockSpec(bias.shape, lambda b, g, i: (0, 0, 0)), cur(W, 0), cur(W, 1), cur(W, 2),
                  prev(1), prev(2)],
        out_specs=[cur(W), cur(LANES)],
        out_shape=[jax.ShapeDtypeStruct((B, r, L, W), BF16), jax.ShapeDtypeStruct((B, r, L, LANES), F32)],
        compiler_params=pltpu.CompilerParams(
            dimension_semantics=("arbitrary", "arbitrary", "arbitrary"), vmem_limit_bytes=VMEM_LIMIT),
        name=f"attn_d{r}",
    )(bias, qkv, qkv, qkv, qkv, qkv)


def _out_kernel(x_ref, p_ref, o1_ref, o4_ref, o16_ref, l1_ref, l4_ref, l16_ref, sg_ref, oa_ref,
                u4_ref, u16_ref, wo32_ref, gple_ref, wg32_ref, bg_ref, wp32_ref, out_ref, wo_ref, wg_ref, wp_ref):
    n_blocks = x_ref.shape[1] // PERM_ROWS

    @pl.when(jnp.logical_and(pl.program_id(0) == 0, pl.program_id(1) == 0))
    def _():
        for src, dst in ((wo32_ref, wo_ref), (wg32_ref, wg_ref), (wp32_ref, wp_ref)):
            for r in range(0, src.shape[0], MXU_DEPTH):
                dst[r:r + MXU_DEPTH, :] = src[r:r + MXU_DEPTH, :].astype(BF16)
    st = [dict() for _ in range(n_blocks)]

    def rows(b):
        return slice(b * PERM_ROWS, (b + 1) * PERM_ROWS)

    def grouped(ref, b):
        r = ref.shape[0]
        n = PERM_ROWS // r
        return jnp.concatenate([ref[c, b * n:(b + 1) * n, :] for c in range(r)], axis=0)

    def to_token_order(b):
        s = st[b]
        s["o4"] = _dot(u4_ref[...], grouped(o4_ref, b))
        s["o16"] = _dot(u16_ref[...], grouped(o16_ref, b))
        for name, ref, u_ref in (("l4", l4_ref, u4_ref), ("l16", l16_ref, u16_ref)):
            cm = grouped(ref, b)
            hi = cm.astype(BF16)
            lo = (cm - hi.astype(F32)).astype(BF16)
            t = _dot(u_ref[...], jnp.concatenate([hi, lo], axis=1))
            s[name] = t[:, :LANES] + t[:, LANES:]

    def merge_weights(b):
        s = st[b]
        l1 = l1_ref[0, rows(b), :]
        m = jnp.maximum(jnp.maximum(l1, s["l4"]), s["l16"])
        e1, e4, e16 = jnp.exp(l1 - m), jnp.exp(s["l4"] - m), jnp.exp(s["l16"] - m)
        inv = 1.0 / (e1 + e4 + e16)
        s["w"] = [e1 * inv, e4 * inv]

    lane = lax.broadcasted_iota(jnp.int32, (1, LANES), 1)

    def spread(w):
        pairs = []
        for p in range(N_PAIRS):
            t = w if p == 0 else pltpu.roll(w, LANES - STAT_REP * p, 1)
            for width in (STAT_REP, 2 * STAT_REP):
                t = jnp.where(lane % (2 * width) < width, t, pltpu.roll(t, width, 1))
            pairs.append(t)
        return jnp.concatenate(pairs, axis=1)

    def expand_weights(b):
        s = st[b]
        s["wf"] = [spread(w) for w in s["w"]]

    def merge(b):
        s = st[b]
        o1 = o1_ref[0, rows(b), :].astype(F32)
        ob = s["o16"] + s["wf"][0] * (o1 - s["o16"]) + s["wf"][1] * (s["o4"] - s["o16"])
        s["mb"] = (ob * sg_ref[0, rows(b), :].astype(F32)).astype(BF16)

    def project(b):
        s = st[b]
        s["t"] = _dot(s["mb"], wo_ref[:ATTN_WIDTH, :]) + _dot(oa_ref[0, rows(b), :], wo_ref[ATTN_WIDTH:, :])

    def residual_norm(b):
        s = st[b]
        h = x_ref[0, rows(b), :] + s["t"]
        s["h"] = h
        s["n"] = (h * lax.rsqrt(jnp.mean(h * h, axis=-1, keepdims=True) + EPS) * gple_ref[...]).astype(BF16)

    def embed_dots(b):
        s = st[b]
        s["g"] = _dot(s["n"], wg_ref[...])
        s["pp"] = _dot(p_ref[0, rows(b), :].astype(BF16), wp_ref[...])

    def embed_out(b):
        s = st[b]
        out_ref[0, rows(b), :] = s["h"] + jax.nn.sigmoid(s["g"] + bg_ref[...]) * s["pp"]

    def merge_branches(b):
        merge_weights(b)
        expand_weights(b)
        merge(b)

    stages = (to_token_order, merge_branches, project, residual_norm, embed_dots, embed_out)
    on_matrix_unit = (to_token_order, project, embed_dots)
    for t in range(len(stages) + n_blocks - 1):
        ready = [(stages[t - b], b) for b in range(n_blocks) if 0 <= t - b < len(stages)]
        for stage, b in sorted(ready, key=lambda sb: sb[0] not in on_matrix_unit):
            stage(b)


def _out_call(x, p, os_, ls_, sg, oa, perms, w_out, g_ple, w_gate, b_gate, w_proj):
    B, S, D = x.shape
    T = OUT_ROWS
    tok = lambda width: pl.BlockSpec((1, T, width), lambda b, i: (b, i, 0))
    cls = lambda r, width: pl.BlockSpec((None, r, T // r, width), lambda b, i: (b, 0, i, 0))
    const = lambda shape: pl.BlockSpec(shape, lambda b, i: (0,) * len(shape))
    r4, r16 = STRIDES
    W = ATTN_WIDTH
    return pl.pallas_call(
        _out_kernel,
        grid=(B, S // T),
        in_specs=[tok(D), tok(PLE_DIM), tok(W), cls(r4, W), cls(r16, W), tok(LANES), cls(r4, LANES),
                  cls(r16, LANES), tok(W), tok(GMLP_WIDTH),
                  const((PERM_ROWS, PERM_ROWS)), const((PERM_ROWS, PERM_ROWS)),
                  const(w_out.shape), const((1, D)), const(w_gate.shape), const((1, D)), const(w_proj.shape)],
        out_specs=tok(D),
        out_shape=jax.ShapeDtypeStruct((B, S, D), x.dtype),
        scratch_shapes=[pltpu.VMEM(w.shape, BF16) for w in (w_out, w_gate, w_proj)],
        compiler_params=pltpu.CompilerParams(
            dimension_semantics=("arbitrary", "arbitrary"), vmem_limit_bytes=VMEM_LIMIT),
        name="out",
    )(x, p, *os_, *ls_, sg, oa, perms[r4][1], perms[r16][1], w_out, g_ple, w_gate, b_gate, w_proj)


def _band_bias():
    qi = np.arange(BLK)[:, None]
    col = np.arange(2 * BLK)[None, :]
    dist = qi + BLK - col
    band = (dist >= 0) & (dist <= BLK)
    both = np.stack([band, band & (col >= BLK)])
    bias = np.where(both, 0.0, NEG).astype(np.float32)
    return jnp.asarray(np.concatenate([bias, bias], axis=1))


def _head_average():
    h = np.arange(ATTN_WIDTH) // HEAD_DIM
    return jnp.asarray((h[:, None] == h[None, :]).astype(np.float32) / HEAD_DIM, dtype=BF16)


def _class_perms(r):
    t = np.arange(PERM_ROWS)
    grouped_pos = (t % r) * (PERM_ROWS // r) + t // r
    d = np.zeros((PERM_ROWS, PERM_ROWS), np.float32)
    d[grouped_pos, t] = 1.0
    return jnp.asarray(d, dtype=BF16), jnp.asarray(d.T, dtype=BF16)


def _inv_freq_col():
    half = HEAD_DIM // 2
    return jnp.exp(-math.log(ROPE_THETA) * jnp.arange(half, dtype=F32) / half)[:, None]


def kernel(x, p, positions, g_in, w_in, q_norm, k_norm, w_spatial, b_spatial, ln_v_g, ln_v_b, w_out,
           g_ple, w_ple_gate, b_ple_gate, w_ple_proj):
    B, S, _ = x.shape
    depth = w_in.shape[0]
    bias = _band_bias()
    avg = _head_average()
    inv_freq = _inv_freq_col()
    perms = {r: _class_perms(r) for r in STRIDES}
    pos_rows = positions.astype(F32).reshape(B, S // CHUNK, CHUNK)
    h = x
    for i in range(depth):
        qkv, qkv4, qkv16, sg, oa = _proj_call(
            h, pos_rows, inv_freq, g_in[i][None, :], w_in[i], avg,
            jnp.tile(q_norm[i], N_HEADS)[None, :], jnp.tile(k_norm[i], N_HEADS)[None, :],
            w_spatial[i], jnp.repeat(b_spatial[i].T, LANES, axis=1),
            ln_v_g[i][None, :], ln_v_b[i][None, :], perms)
        o1, l1 = _attn_call(bias, qkv[:, None])
        o4, l4 = _attn_call(bias, qkv4)
        o16, l16 = _attn_call(bias, qkv16)
        h = _out_call(h, p[i], (o1[:, 0], o4, o16), (l1[:, 0], l4, l16), sg, oa, perms,
                      w_out[i], g_ple[i][None, :], w_ple_gate[i], b_ple_gate[i][None, :], w_ple_proj[i])
    return h
```
